```python
import jax, jax.numpy as jnp
from jax import lax
import numpy as np

D_MODEL = 1024
BATCH = 1
SEQ = 16384
DEPTH = 2

CHUNK = 64
PLE_DIM = 256
HG_HEADS = 4
HG_DK = 128
HG_DV = 128
HG_WIDTH = HG_HEADS * HG_DV
SB_HEADS = 8
SB_DH = 64
SB_WIDTH = SB_HEADS * SB_DH
Q_BLOCK = 128
D_FF = 2816
CONV_W = 3
DN_ALPHA = (2 * DEPTH) ** 0.25
DN_BETA = (8 * DEPTH) ** -0.25
LN_EPS = 1e-5
RMS_EPS = 1e-6
F_FLOOR = 1e-30

SPLITS = (HG_HEADS * HG_DK, HG_HEADS * HG_DK, HG_WIDTH, HG_WIDTH,
          SB_WIDTH, SB_WIDTH, SB_WIDTH, D_MODEL, D_MODEL)
IN_COLS = int(sum(SPLITS))
SPLIT_IDX = tuple(int(v) for v in np.cumsum(SPLITS)[:-1])

kernel_name = "hgrn2_stickbreaking_gated_hybrid_deepnorm"


def layer_norm(x, g, b):
    x32 = x.astype(jnp.float32)
    mu = jnp.mean(x32, axis=-1, keepdims=True)
    var = jnp.mean(jnp.square(x32 - mu), axis=-1, keepdims=True)
    y = (x32 - mu) * lax.rsqrt(var + LN_EPS) * g.astype(jnp.float32) + b.astype(jnp.float32)
    return y.astype(x.dtype)


def rms_norm(x, g):
    x32 = x.astype(jnp.float32)
    return x32 * lax.rsqrt(jnp.mean(jnp.square(x32), axis=-1, keepdims=True) + RMS_EPS) * g.astype(jnp.float32)


def hgrn2_recurrence(q, f_logit, v, lb):
    B, S, H, Dk = q.shape
    Dv = v.shape[-1]
    n = S // CHUNK
    f32 = jnp.float32
    z = f_logit.astype(f32)
    lb = lb.astype(f32)
    q = jax.nn.silu(q.astype(f32))
    f_gate = lb + (1.0 - lb) * jax.nn.sigmoid(z)
    log_f = jnp.log(jnp.maximum(f_gate, F_FLOOR))
    k = (1.0 - lb) * jax.nn.sigmoid(-z)
    v = v.astype(f32)

    def to_chunks(a):
        return jnp.moveaxis(a.reshape(B, n, CHUNK, *a.shape[2:]), 1, 0)

    causal = jnp.tril(jnp.ones((CHUNK, CHUNK), dtype=bool))[None, :, :, None, None]

    def step(state, inp):
        qc, kc, vc, gc = inp
        b = jnp.cumsum(gc, axis=1)
        o_inter = jnp.einsum('bthk,bhkv->bthv', qc * jnp.exp(b), state)
        diff = jnp.where(causal, b[:, :, None] - b[:, None, :], 0.0)
        decay = jnp.where(causal, jnp.exp(diff), 0.0)
        scores = jnp.einsum('bthk,bshk,btshk->btsh', qc, kc, decay)
        o_intra = jnp.einsum('btsh,bshv->bthv', scores, vc)
        b_last = b[:, -1]
        k_dec = kc * jnp.exp(b_last[:, None] - b)
        state = state * jnp.exp(b_last)[..., None] + jnp.einsum('bshk,bshv->bhkv', k_dec, vc)
        return state, o_inter + o_intra

    state0 = jnp.zeros((B, H, Dk, Dv), f32)
    _, o = lax.scan(step, state0, (to_chunks(q), to_chunks(k), to_chunks(v), to_chunks(log_f)))
    return jnp.moveaxis(o, 0, 1).reshape(B, S, H, Dv)


def stick_breaking_attention(q, k, v):
    B, S, H, Dh = q.shape
    nblk = S // Q_BLOCK
    qh = jnp.transpose(q, (0, 2, 1, 3)) * (Dh ** -0.5)
    kh = jnp.transpose(k, (0, 2, 1, 3))
    vh = jnp.transpose(v, (0, 2, 1, 3)).astype(jnp.float32)
    q_blocks = jnp.moveaxis(qh.reshape(B, H, nblk, Q_BLOCK, Dh), 2, 0)
    key_pos = jnp.arange(S)

    def block(args):
        qi, bi = args
        z = jnp.einsum('bhtd,bhsd->bhts', qi, kh).astype(jnp.float32)
        t_pos = bi * Q_BLOCK + jnp.arange(Q_BLOCK)
        mask = (key_pos[None, :] < t_pos[:, None])[None, None]
        log_fail = jnp.where(mask, jax.nn.log_sigmoid(-z), 0.0)
        suffix = lax.cumsum(log_fail, axis=3, reverse=True) - log_fail
        log_w = jnp.where(mask, jax.nn.log_sigmoid(z) + suffix, 0.0)
        w = jnp.where(mask, jnp.exp(log_w), 0.0)
        return jnp.einsum('bhts,bhsd->bhtd', w, vh)

    out = lax.map(block, (q_blocks, jnp.arange(nblk)))
    return jnp.transpose(out, (1, 0, 3, 2, 4)).reshape(B, S, H * Dh)


def causal_depthwise_conv(h, w, b):
    S = h.shape[1]
    hp = jnp.pad(h, ((0, 0), (CONV_W - 1, 0), (0, 0)))
    out = b
    for j in range(CONV_W):
        out = out + hp[:, j:j + S] * w[j]
    return out


def hybrid_layer(x, p_i, lb, w_in, hg_norm_g, w_a, w_b, w_out, ln1_g, ln1_b,
                 w_up, conv_w, conv_b, w_down, w_pe, w_pg, ln2_g, ln2_b):
    B, S, _ = x.shape
    dt = x.dtype
    proj = x @ w_in
    qa, fa, ia, ga, qb, kb, vb, gate_a, gate_b = jnp.split(proj, SPLIT_IDX, axis=-1)

    oa = hgrn2_recurrence(qa.reshape(B, S, HG_HEADS, HG_DK), fa.reshape(B, S, HG_HEADS, HG_DK),
                          ia.reshape(B, S, HG_HEADS, HG_DV), lb.reshape(HG_HEADS, HG_DK))
    oa = rms_norm(oa, hg_norm_g.reshape(HG_HEADS, HG_DV)) * jax.nn.silu(
        ga.reshape(B, S, HG_HEADS, HG_DV).astype(jnp.float32))
    ya = oa.reshape(B, S, HG_WIDTH).astype(dt) @ w_a

    ob = stick_breaking_attention(qb.reshape(B, S, SB_HEADS, SB_DH), kb.reshape(B, S, SB_HEADS, SB_DH),
                                  vb.reshape(B, S, SB_HEADS, SB_DH))
    yb = ob.astype(dt) @ w_b

    merged = jax.nn.sigmoid(gate_a) * ya + jax.nn.sigmoid(gate_b) * yb
    x1 = layer_norm(DN_ALPHA * x + merged @ w_out, ln1_g, ln1_b)

    up = causal_depthwise_conv(x1 @ w_up, conv_w, conv_b)
    c_val, c_gate = jnp.split(up, 2, axis=-1)
    ffn = (jax.nn.gelu(c_gate) * c_val) @ w_down

    ple = (p_i @ w_pe) * jax.nn.sigmoid(x1 @ w_pg)
    return layer_norm(DN_ALPHA * x1 + ffn + ple, ln2_g, ln2_b)


def setup_inputs(seed: int = 0) -> dict:
    key = jax.random.key(seed)
    ks = jax.random.split(key, 20)
    f32 = jnp.float32
    nrm = lambda k, shape, s: jax.random.normal(k, shape, f32) * s
    return {
        "x": nrm(ks[0], (BATCH, SEQ, D_MODEL), 1.0),
        "p": nrm(ks[1], (DEPTH, BATCH, SEQ, PLE_DIM), 1.0),
        "lb_logits": 1.0 + nrm(ks[2], (DEPTH, HG_HEADS * HG_DK), 0.1),
        "w_in": nrm(ks[3], (DEPTH, D_MODEL, IN_COLS), D_MODEL ** -0.5),
        "hg_norm_g": 1.0 + nrm(ks[4], (DEPTH, HG_WIDTH), 0.02),
        "w_a": nrm(ks[5], (DEPTH, HG_WIDTH, D_MODEL), HG_WIDTH ** -0.5),
        "w_b": nrm(ks[6], (DEPTH, SB_WIDTH, D_MODEL), SB_WIDTH ** -0.5),
        "w_out": nrm(ks[7], (DEPTH, D_MODEL, D_MODEL), DN_BETA * D_MODEL ** -0.5),
        "ln1_g": 1.0 + nrm(ks[8], (DEPTH, D_MODEL), 0.02),
        "ln1_b": nrm(ks[9], (DEPTH, D_MODEL), 0.02),
        "w_up": nrm(ks[10], (DEPTH, D_MODEL, 2 * D_FF), D_MODEL ** -0.5),
        "conv_w": nrm(ks[11], (DEPTH, CONV_W, 2 * D_FF), CONV_W ** -0.5),
        "conv_b": nrm(ks[12], (DEPTH, 2 * D_FF), 0.02),
        "w_down": nrm(ks[13], (DEPTH, D_FF, D_MODEL), DN_BETA * D_FF ** -0.5),
        "w_pe": nrm(ks[14], (DEPTH, PLE_DIM, D_MODEL), PLE_DIM ** -0.5),
        "w_pg": nrm(ks[15], (DEPTH, D_MODEL, D_MODEL), D_MODEL ** -0.5),
        "ln2_g": 1.0 + nrm(ks[16], (DEPTH, D_MODEL), 0.02),
        "ln2_b": nrm(ks[17], (DEPTH, D_MODEL), 0.02),
    }


def reference(x, p, lb_logits, w_in, hg_norm_g, w_a, w_b, w_out, ln1_g, ln1_b,
              w_up, conv_w, conv_b, w_down, w_pe, w_pg, ln2_g, ln2_b):
    sm = jax.nn.softmax(lb_logits.astype(jnp.float32), axis=0)
    lower_bounds = jnp.cumsum(sm, axis=0) - sm[0]
    h = x
    for i in range(DEPTH):
        h = hybrid_layer(h, p[i], lower_bounds[i], w_in[i], hg_norm_g[i], w_a[i], w_b[i], w_out[i],
                         ln1_g[i], ln1_b[i], w_up[i], conv_w[i], conv_b[i], w_down[i],
                         w_pe[i], w_pg[i], ln2_g[i], ln2_b[i])
    return h
```

```python
import functools

import jax
import jax.numpy as jnp
from jax import lax
from jax.experimental import pallas as pl
from jax.experimental.pallas import tpu as pltpu

F32 = jnp.float32
BF16 = jnp.bfloat16

D_MODEL = 1024
PLE_DIM = 256
HG_HEADS = 4
HG_DK = 128
HG_DV = 128
HG_WIDTH = HG_HEADS * HG_DV
SB_HEADS = 8
SB_DH = 64
SB_WIDTH = SB_HEADS * SB_DH
D_FF = 2816
CONV_W = 3
LN_EPS = 1e-5
RMS_EPS = 1e-6
F_FLOOR = 1e-30

_OFF_QA = 0
_OFF_FA = _OFF_QA + HG_HEADS * HG_DK
_OFF_IA = _OFF_FA + HG_HEADS * HG_DK
_OFF_GA = _OFF_IA + HG_WIDTH
_OFF_QB = _OFF_GA + HG_WIDTH
_OFF_KB = _OFF_QB + SB_WIDTH
_OFF_VB = _OFF_KB + SB_WIDTH
_OFF_GATE_A = _OFF_VB + SB_WIDTH
_OFF_GATE_B = _OFF_GATE_A + D_MODEL
IN_COLS = _OFF_GATE_B + D_MODEL

HG_CHUNK = 64
HG_SUB = 16
SB_BLOCK = 128
LANES = 128
SB_LOG_ZERO = -110.0
_SUB_SHIFT = HG_SUB.bit_length() - 1
_DH_SHIFT = SB_DH.bit_length() - 1

VMEM_LIMIT = 56 * 1024 * 1024


def _params(*sem):
    return pltpu.CompilerParams(dimension_semantics=sem, vmem_limit_bytes=VMEM_LIMIT)


def _const_spec(shape):
    nd = len(shape)
    return pl.BlockSpec(shape, lambda *_: (0,) * nd, pipeline_mode=pl.Buffered(1))


def _split2(a):
    hi = a.astype(BF16)
    lo = (a - hi.astype(F32)).astype(BF16)
    return hi, lo


def _dot(a, b):
    return jnp.dot(a, b, preferred_element_type=F32)


def _dot_nt(a, b):
    return lax.dot_general(a, b, (((1,), (1,)), ((), ())), preferred_element_type=F32)


def _dot_tn(a, b):
    return lax.dot_general(a, b, (((0,), (0,)), ((), ())), preferred_element_type=F32)


def _dot3(a, b, dot):
    ah, al = _split2(a)
    bh, bl = _split2(b)
    return dot(ah, bh) + dot(al, bh) + dot(ah, bl)


def _layer_norm(y, g, b):
    mu = jnp.mean(y, axis=-1, keepdims=True)
    yc = y - mu
    var = jnp.mean(yc * yc, axis=-1, keepdims=True)
    return yc * lax.rsqrt(var + LN_EPS) * g + b


def _inproj_kernel(layer, x_ref, w_ref, lbl_ref,
                   qa_ref, g_ref, kk_ref, ia_ref, ga_ref,
                   qb_ref, kbh_ref, kbl_ref, vbh_ref, vbl_ref, sga_ref, sgb_ref):
    xb = x_ref[...].astype(BF16)

    def proj(c0, width):
        return _dot(xb, w_ref[:, c0:c0 + width])

    logits = lbl_ref[...]
    m = jnp.max(logits, axis=0, keepdims=True)
    e = jnp.exp(logits - m)
    sm = e / jnp.sum(e, axis=0, keepdims=True)
    csum = sm[0:1]
    for j in range(1, layer + 1):
        csum = csum + sm[j:j + 1]
    lb = csum - sm[0:1]

    qa_ref[...] = jax.nn.silu(proj(_OFF_QA, HG_WIDTH))
    z = proj(_OFF_FA, HG_WIDTH)
    f_gate = lb + (1.0 - lb) * jax.nn.sigmoid(z)
    g_ref[...] = jnp.log(jnp.maximum(f_gate, F_FLOOR))
    kk_ref[...] = (1.0 - lb) * jax.nn.sigmoid(-z)
    ia_ref[...] = proj(_OFF_IA, HG_WIDTH)
    ga_ref[...] = jax.nn.silu(proj(_OFF_GA, HG_WIDTH))

    qb_ref[...] = proj(_OFF_QB, SB_WIDTH) * (SB_DH ** -0.5)
    kh, kl = _split2(proj(_OFF_KB, SB_WIDTH))
    kbh_ref[...] = kh
    kbl_ref[...] = kl
    vh, vl = _split2(proj(_OFF_VB, SB_WIDTH))
    vbh_ref[...] = vh
    vbl_ref[...] = vl
    sga_ref[...] = jax.nn.sigmoid(proj(_OFF_GATE_A, D_MODEL))
    sgb_ref[...] = jax.nn.sigmoid(proj(_OFF_GATE_B, D_MODEL))


def _inproj(x, w_in, lb_logits, layer, ts):
    s = x.shape[0]
    depth = lb_logits.shape[0]
    row = lambda w: pl.BlockSpec((ts, w), lambda i: (i, 0))
    f32o = lambda w: jax.ShapeDtypeStruct((s, w), F32)
    bf16o = lambda w: jax.ShapeDtypeStruct((s, w), BF16)
    return pl.pallas_call(
        functools.partial(_inproj_kernel, layer),
        grid=(s // ts,),
        in_specs=[row(D_MODEL), _const_spec((D_MODEL, IN_COLS)), _const_spec((depth, HG_WIDTH))],
        out_specs=[row(HG_WIDTH)] * 5 + [row(SB_WIDTH)] * 5 + [row(D_MODEL)] * 2,
        out_shape=[f32o(HG_WIDTH)] * 5 + [f32o(SB_WIDTH)] + [bf16o(SB_WIDTH)] * 4 + [f32o(D_MODEL)] * 2,
        compiler_params=_params("arbitrary"),
        name="inproj",
    )(x, w_in, lb_logits)


def _hgrn_chunk(q, g, k, v, st):
    c = HG_CHUNK
    nsub = c // HG_SUB
    row = lax.broadcasted_iota(jnp.int32, (c, c), 0)
    col = lax.broadcasted_iota(jnp.int32, (c, c), 1)

    tri = jnp.where(col <= row, 1.0, 0.0).astype(BF16)
    gh, gl = _split2(g)
    b = _dot(tri, gh) + _dot(tri, gl)
    b_last = b[c - 1:c, :]

    o = _dot3(q * jnp.exp(b), st, _dot_nt)

    subr = lax.broadcasted_iota(jnp.int32, (c, HG_DK), 0) >> _SUB_SHIFT
    ends = [b[(j + 1) * HG_SUB - 1:(j + 1) * HG_SUB, :] for j in range(nsub)]
    e_own = ends[nsub - 1]
    for j in range(nsub - 2, -1, -1):
        e_own = jnp.where(subr == j, ends[j], e_own)
    khat = k * jnp.exp(jnp.minimum(e_own - b, 0.0))
    qcat = jnp.concatenate(
        [q * jnp.exp(jnp.minimum(b - ends[j], 0.0)) for j in range(nsub - 1)], axis=1)
    kcat = jnp.concatenate(
        [jnp.where(subr == j, khat, 0.0) for j in range(nsub - 1)], axis=1)
    a_off = _dot3(qcat, kcat, _dot_nt)

    a_diag = jnp.zeros((c, c), F32)
    row_in_sub = row & (HG_SUB - 1)
    for d in range(HG_SUB):
        if d == 0:
            term = q * k
        else:
            bs = pltpu.roll(b, d, axis=0)
            ks = pltpu.roll(k, d, axis=0)
            term = q * ks * jnp.exp(jnp.minimum(b - bs, 0.0))
        a_d = jnp.sum(term, axis=-1, keepdims=True)
        a_diag = jnp.where((col == row - d) & (row_in_sub >= d), a_d, a_diag)

    a = jnp.where((row >> _SUB_SHIFT) > (col >> _SUB_SHIFT), a_off, a_diag)
    o = o + _dot3(a, v, _dot)

    kdec = k * jnp.exp(b_last - b)
    st_new = st * jnp.exp(b_last) + _dot3(v, kdec, _dot_tn)
    return o, st_new


def _hgrn_kernel(q_ref, g_ref, k_ref, v_ref, ga_ref, gn_ref, o_ref, st_ref):
    @pl.when(pl.program_id(1) == 0)
    def _():
        st_ref[...] = jnp.zeros_like(st_ref)

    gn = gn_ref[...]
    for ci in range(q_ref.shape[0] // HG_CHUNK):
        rows = pl.ds(ci * HG_CHUNK, HG_CHUNK)
        o, st_new = _hgrn_chunk(q_ref[rows, :], g_ref[rows, :], k_ref[rows, :], v_ref[rows, :],
                                st_ref[...])
        st_ref[...] = st_new
        ms = jnp.mean(o * o, axis=-1, keepdims=True)
        o_ref[rows, :] = o * lax.rsqrt(ms + RMS_EPS) * gn * ga_ref[rows, :]


def _hgrn(qa, g, kk, ia, ga, gn, tb):
    s = qa.shape[0]
    blk = pl.BlockSpec((tb, HG_DK), lambda h, c: (c, h))
    return pl.pallas_call(
        _hgrn_kernel,
        grid=(HG_HEADS, s // tb),
        in_specs=[blk] * 5 + [pl.BlockSpec((1, HG_DV), lambda h, c: (0, h))],
        out_specs=blk,
        out_shape=jax.ShapeDtypeStruct((s, HG_WIDTH), F32),
        scratch_shapes=[pltpu.VMEM((HG_DV, HG_DK), F32)],
        compiler_params=_params("arbitrary", "arbitrary"),
        name="hgrn2",
    )(qa, g, kk, ia, ga, gn)


def _sb_block(lhs, kh, kl, vh, vl, lane_mask, carry, tri_ones, mask):
    z = _dot_nt(lhs, jnp.concatenate([kh, kl, kh], axis=1))
    l1p = jnp.log1p(jnp.exp(-jnp.abs(z)))
    log_fail = -(jnp.maximum(z, 0.0) + l1p)
    log_beta = jnp.minimum(z, 0.0) - l1p
    if mask is not None:
        log_fail = jnp.where(mask, log_fail, 0.0)
    fh, fl = _split2(log_fail)
    sums = _dot(jnp.concatenate([fh, fl], axis=1), tri_ones)
    suffix = sums[:, :SB_BLOCK]
    total = sums[:, SB_BLOCK:]
    w = jnp.exp(log_beta + suffix + carry)
    if mask is not None:
        w = jnp.where(mask, w, 0.0)
    wh, wl = _split2(w)
    vhm = jnp.where(lane_mask, vh, jnp.zeros_like(vh))
    vlm = jnp.where(lane_mask, vl, jnp.zeros_like(vl))
    pv = _dot(wh, vhm) + _dot(wl, vhm) + _dot(wh, vlm)
    return pv, total


def _sb_kernel(q_ref, kh_ref, kl_ref, vh_ref, vl_ref, o_ref):
    i = pl.program_id(1)
    tq = SB_BLOCK
    q = q_ref[...]
    lane = lax.broadcasted_iota(jnp.int32, (tq, LANES), 1)
    row = lax.broadcasted_iota(jnp.int32, (tq, tq), 0)
    col = lax.broadcasted_iota(jnp.int32, (tq, tq), 1)
    causal = col < row
    r2 = lax.broadcasted_iota(jnp.int32, (2 * tq, 2 * tq), 0) & (tq - 1)
    c2 = lax.broadcasted_iota(jnp.int32, (2 * tq, 2 * tq), 1)
    tri_ones = jnp.where((c2 >= tq) | (r2 > c2), 1.0, 0.0).astype(BF16)

    out = jnp.zeros((tq, LANES), F32)
    for h in range(LANES // SB_DH):
        lane_mask = (lane >> _DH_SHIFT) == h
        qh, ql = _split2(jnp.where(lane_mask, q, 0.0))
        lhs = jnp.concatenate([qh, qh, ql], axis=1)

        def tile(j, carry, mask):
            rows = pl.ds(pl.multiple_of(j * tq, tq), tq)
            return _sb_block(lhs, kh_ref[rows, :], kl_ref[rows, :], vh_ref[rows, :], vl_ref[rows, :],
                             lane_mask, carry, tri_ones, mask)

        acc, carry = tile(i, jnp.zeros((tq, LANES), F32), causal)

        def cond(state):
            j, _, _, cmax = state
            return (j >= 0) & (cmax > SB_LOG_ZERO)

        def body(state):
            j, acc, carry, _ = state
            pv, total = tile(j, carry, None)
            carry = carry + total
            return j - 1, acc + pv, carry, jnp.max(carry)

        _, acc, _, _ = lax.while_loop(cond, body, (i - 1, acc, carry, jnp.max(carry)))
        out = out + acc
    o_ref[...] = out


def _sb_attention(qb, kbh, kbl, vbh, vbl):
    s = qb.shape[0]
    kv = pl.BlockSpec((s, LANES), lambda p, i: (0, p))
    qo = pl.BlockSpec((SB_BLOCK, LANES), lambda p, i: (i, p))
    return pl.pallas_call(
        _sb_kernel,
        grid=(SB_WIDTH // LANES, s // SB_BLOCK),
        in_specs=[qo, kv, kv, kv, kv],
        out_specs=qo,
        out_shape=jax.ShapeDtypeStruct((s, SB_WIDTH), F32),
        compiler_params=_params("arbitrary", "arbitrary"),
        name="stickbreaking",
    )(qb, kbh, kbl, vbh, vbl)


def _merge_kernel(alpha, x_ref, oa_ref, ob_ref, sga_ref, sgb_ref, wa_ref, wb_ref, wo_ref,
                  g_ref, b_ref, o_ref):
    ya = _dot(oa_ref[...].astype(BF16), wa_ref[...])
    yb = _dot(ob_ref[...].astype(BF16), wb_ref[...])
    merged = sga_ref[...] * ya + sgb_ref[...] * yb
    y = alpha * x_ref[...] + _dot(merged.astype(BF16), wo_ref[...])
    o_ref[...] = _layer_norm(y, g_ref[...], b_ref[...])


def _merge(x, oa, ob, sga, sgb, w_a, w_b, w_out, ln_g, ln_b, alpha, ts):
    s = x.shape[0]
    row = lambda w: pl.BlockSpec((ts, w), lambda i: (i, 0))
    return pl.pallas_call(
        functools.partial(_merge_kernel, alpha),
        grid=(s // ts,),
        in_specs=[row(D_MODEL), row(HG_WIDTH), row(SB_WIDTH), row(D_MODEL), row(D_MODEL),
                  _const_spec((HG_WIDTH, D_MODEL)), _const_spec((SB_WIDTH, D_MODEL)),
                  _const_spec((D_MODEL, D_MODEL)), _const_spec((1, D_MODEL)), _const_spec((1, D_MODEL))],
        out_specs=row(D_MODEL),
        out_shape=jax.ShapeDtypeStruct((s, D_MODEL), F32),
        compiler_params=_params("arbitrary"),
        name="merge_ln1",
    )(x, oa, ob, sga, sgb, w_a, w_b, w_out, ln_g, ln_b)


FF_CHUNK = 256
CARRY_ROWS = 8


def _ffn_kernel(alpha, x_ref, p_ref, wup_ref, cw_ref, cb_ref, wdn_ref, wpe_ref, wpg_ref,
                g_ref, b_ref, o_ref, carry_ref):
    ts = x_ref.shape[0]

    @pl.when(pl.program_id(0) == 0)
    def _():
        carry_ref[...] = jnp.zeros_like(carry_ref)

    x1 = x_ref[...]
    xb = x1.astype(BF16)
    row = lax.broadcasted_iota(jnp.int32, (ts, FF_CHUNK), 0)

    def conv_up(c0):
        cols = slice(c0, c0 + FF_CHUNK)
        u = _dot(xb, wup_ref[:, cols])
        prev = carry_ref[:, cols]
        p1 = prev[CARRY_ROWS - 1:CARRY_ROWS, :]
        p2 = prev[CARRY_ROWS - 2:CARRY_ROWS - 1, :]
        u1 = jnp.where(row == 0, p1, pltpu.roll(u, 1, axis=0))
        u2 = jnp.where(row == 0, p2, jnp.where(row == 1, p1, pltpu.roll(u, 2, axis=0)))
        carry_ref[:, cols] = u[ts - CARRY_ROWS:, :]
        cw = cw_ref[:, cols]
        return cb_ref[:, cols] + cw[0:1, :] * u2 + cw[1:2, :] * u1 + cw[2:3, :] * u

    ffn = jnp.zeros((ts, D_MODEL), F32)
    for c in range(D_FF // FF_CHUNK):
        c_val = conv_up(c * FF_CHUNK)
        c_gate = conv_up(D_FF + c * FF_CHUNK)
        h = (jax.nn.gelu(c_gate) * c_val).astype(BF16)
        ffn = ffn + _dot(h, wdn_ref[c * FF_CHUNK:(c + 1) * FF_CHUNK, :])

    ple = _dot(p_ref[...].astype(BF16), wpe_ref[...]) * jax.nn.sigmoid(_dot(xb, wpg_ref[...]))
    o_ref[...] = _layer_norm(alpha * x1 + ffn + ple, g_ref[...], b_ref[...])


def _ffn(x1, p, w_up, conv_w, conv_b, w_down, w_pe, w_pg, ln_g, ln_b, alpha, ts):
    s = x1.shape[0]
    row = lambda w: pl.BlockSpec((ts, w), lambda i: (i, 0))
    return pl.pallas_call(
        functools.partial(_ffn_kernel, alpha),
        grid=(s // ts,),
        in_specs=[row(D_MODEL), row(PLE_DIM),
                  _const_spec((D_MODEL, 2 * D_FF)), _const_spec((CONV_W, 2 * D_FF)),
                  _const_spec((1, 2 * D_FF)), _const_spec((D_FF, D_MODEL)),
                  _const_spec((PLE_DIM, D_MODEL)), _const_spec((D_MODEL, D_MODEL)),
                  _const_spec((1, D_MODEL)), _const_spec((1, D_MODEL))],
        out_specs=row(D_MODEL),
        out_shape=jax.ShapeDtypeStruct((s, D_MODEL), F32),
        scratch_shapes=[pltpu.VMEM((CARRY_ROWS, 2 * D_FF), F32)],
        compiler_params=_params("arbitrary"),
        name="convffn_ln2",
    )(x1, p, w_up, conv_w, conv_b, w_down, w_pe, w_pg, ln_g, ln_b)


def _tiles(s):
    ts = 256 if s % 256 == 0 else SB_BLOCK
    return ts, ts


def kernel(x, p, lb_logits, w_in, hg_norm_g, w_a, w_b, w_out, ln1_g, ln1_b,
           w_up, conv_w, conv_b, w_down, w_pe, w_pg, ln2_g, ln2_b):
    batch, s, _ = x.shape
    depth = w_in.shape[0]
    assert s % SB_BLOCK == 0 and s % HG_CHUNK == 0
    alpha = float((2 * depth) ** 0.25)
    ts, tb = _tiles(s)
    lbl = lb_logits.astype(F32)
    outs = []
    for bi in range(batch):
        h = x[bi]
        for i in range(depth):
            (qa, g, kk, ia, ga, qb, kbh, kbl, vbh, vbl, sga, sgb) = _inproj(
                h, w_in[i].astype(BF16), lbl, i, ts)
            oa = _hgrn(qa, g, kk, ia, ga, hg_norm_g[i].reshape(1, HG_WIDTH), tb)
            ob = _sb_attention(qb, kbh, kbl, vbh, vbl)
            x1 = _merge(h, oa, ob, sga, sgb, w_a[i].astype(BF16), w_b[i].astype(BF16),
                        w_out[i].astype(BF16), ln1_g[i].reshape(1, D_MODEL),
                        ln1_b[i].reshape(1, D_MODEL), alpha, ts)
            h = _ffn(x1, p[i, bi], w_up[i].astype(BF16), conv_w[i], conv_b[i].reshape(1, 2 * D_FF),
                     w_down[i].astype(BF16), w_pe[i].astype(BF16), w_pg[i].astype(BF16),
                     ln2_g[i].reshape(1, D_MODEL), ln2_b[i].reshape(1, D_MODEL), alpha, ts)
        outs.append(h)
    return jnp.stack(outs, axis=0)
```

```python
import functools

import jax
import jax.numpy as jnp
from jax import lax
from jax.experimental import pallas as pl
from jax.experimental.pallas import tpu as pltpu

F32 = jnp.float32
BF16 = jnp.bfloat16

D_MODEL = 1024
PLE_DIM = 256
HG_HEADS = 4
HG_DK = 128
HG_DV = 128
HG_WIDTH = HG_HEADS * HG_DV
SB_HEADS = 8
SB_DH = 64
SB_WIDTH = SB_HEADS * SB_DH
D_FF = 2816
CONV_W = 3
LN_EPS = 1e-5
RMS_EPS = 1e-6
F_FLOOR = 1e-30

_OFF_QA = 0
_OFF_FA = _OFF_QA + HG_HEADS * HG_DK
_OFF_IA = _OFF_FA + HG_HEADS * HG_DK
_OFF_GA = _OFF_IA + HG_WIDTH
_OFF_QB = _OFF_GA + HG_WIDTH
_OFF_KB = _OFF_QB + SB_WIDTH
_OFF_VB = _OFF_KB + SB_WIDTH
_OFF_GATE_A = _OFF_VB + SB_WIDTH
_OFF_GATE_B = _OFF_GATE_A + D_MODEL
IN_COLS = _OFF_GATE_B + D_MODEL

HG_CHUNK = 64
HG_SUB = 16
SB_BLOCK = 128
LANES = 128
SB_LOG_ZERO = -110.0
_SUB_SHIFT = HG_SUB.bit_length() - 1
_DH_SHIFT = SB_DH.bit_length() - 1

VMEM_LIMIT = 56 * 1024 * 1024


def _params(*sem):
    return pltpu.CompilerParams(dimension_semantics=sem, vmem_limit_bytes=VMEM_LIMIT)


def _const_spec(shape):
    nd = len(shape)
    return pl.BlockSpec(shape, lambda *_: (0,) * nd, pipeline_mode=pl.Buffered(1))


def _split2(a):
    hi = a.astype(BF16)
    lo = (a - hi.astype(F32)).astype(BF16)
    return hi, lo


def _dot(a, b):
    return jnp.dot(a, b, preferred_element_type=F32)


def _dot_nt(a, b):
    return lax.dot_general(a, b, (((1,), (1,)), ((), ())), preferred_element_type=F32)


def _dot_tn(a, b):
    return lax.dot_general(a, b, (((0,), (0,)), ((), ())), preferred_element_type=F32)


def _dot3(a, b, dot):
    ah, al = _split2(a)
    bh, bl = _split2(b)
    return dot(ah, bh) + dot(al, bh) + dot(ah, bl)


def _layer_norm(y, g, b):
    mu = jnp.mean(y, axis=-1, keepdims=True)
    yc = y - mu
    var = jnp.mean(yc * yc, axis=-1, keepdims=True)
    return yc * lax.rsqrt(var + LN_EPS) * g + b


def _inproj_kernel(layer, x_ref, w_ref, lbl_ref,
                   qa_ref, g_ref, kk_ref, ia_ref, ga_ref,
                   qb_ref, kb_ref, vb_ref, sga_ref, sgb_ref):
    xb = x_ref[...].astype(BF16)

    def proj(c0, width):
        return _dot(xb, w_ref[:, c0:c0 + width])

    logits = lbl_ref[...]
    m = jnp.max(logits, axis=0, keepdims=True)
    e = jnp.exp(logits - m)
    sm = e / jnp.sum(e, axis=0, keepdims=True)
    csum = sm[0:1]
    for j in range(1, layer + 1):
        csum = csum + sm[j:j + 1]
    lb = csum - sm[0:1]

    qa_ref[...] = jax.nn.silu(proj(_OFF_QA, HG_WIDTH))
    z = proj(_OFF_FA, HG_WIDTH)
    f_gate = lb + (1.0 - lb) * jax.nn.sigmoid(z)
    g_ref[...] = jnp.log(jnp.maximum(f_gate, F_FLOOR))
    kk_ref[...] = (1.0 - lb) * jax.nn.sigmoid(-z)
    ia_ref[...] = proj(_OFF_IA, HG_WIDTH)
    ga_ref[...] = jax.nn.silu(proj(_OFF_GA, HG_WIDTH))

    qb_ref[...] = (proj(_OFF_QB, SB_WIDTH) * (SB_DH ** -0.5)).astype(BF16)
    kb_ref[...] = proj(_OFF_KB, SB_WIDTH).astype(BF16)
    vb_ref[...] = proj(_OFF_VB, SB_WIDTH).astype(BF16)
    sga_ref[...] = jax.nn.sigmoid(proj(_OFF_GATE_A, D_MODEL))
    sgb_ref[...] = jax.nn.sigmoid(proj(_OFF_GATE_B, D_MODEL))


def _inproj(x, w_in, lb_logits, layer, ts):
    s = x.shape[0]
    depth = lb_logits.shape[0]
    row = lambda w: pl.BlockSpec((ts, w), lambda i: (i, 0))
    f32o = lambda w: jax.ShapeDtypeStruct((s, w), F32)
    bf16o = lambda w: jax.ShapeDtypeStruct((s, w), BF16)
    return pl.pallas_call(
        functools.partial(_inproj_kernel, layer),
        grid=(s // ts,),
        in_specs=[row(D_MODEL), _const_spec((D_MODEL, IN_COLS)), _const_spec((depth, HG_WIDTH))],
        out_specs=[row(HG_WIDTH)] * 5 + [row(SB_WIDTH)] * 3 + [row(D_MODEL)] * 2,
        out_shape=[f32o(HG_WIDTH)] * 5 + [bf16o(SB_WIDTH)] * 3 + [f32o(D_MODEL)] * 2,
        compiler_params=_params("arbitrary"),
        name="inproj",
    )(x, w_in, lb_logits)


def _hgrn_chunk(q, g, k, v, st):
    c = HG_CHUNK
    nsub = c // HG_SUB
    row = lax.broadcasted_iota(jnp.int32, (c, c), 0)
    col = lax.broadcasted_iota(jnp.int32, (c, c), 1)

    tri = jnp.where(col <= row, 1.0, 0.0).astype(BF16)
    gh, gl = _split2(g)
    b = _dot(tri, gh) + _dot(tri, gl)
    b_last = b[c - 1:c, :]

    o = _dot3(q * jnp.exp(b), st, _dot_nt)

    subr = lax.broadcasted_iota(jnp.int32, (c, HG_DK), 0) >> _SUB_SHIFT
    ends = [b[(j + 1) * HG_SUB - 1:(j + 1) * HG_SUB, :] for j in range(nsub)]
    e_own = ends[nsub - 1]
    for j in range(nsub - 2, -1, -1):
        e_own = jnp.where(subr == j, ends[j], e_own)
    khat = k * jnp.exp(jnp.minimum(e_own - b, 0.0))
    qcat = jnp.concatenate(
        [q * jnp.exp(jnp.minimum(b - ends[j], 0.0)) for j in range(nsub - 1)], axis=1)
    kcat = jnp.concatenate(
        [jnp.where(subr == j, khat, 0.0) for j in range(nsub - 1)], axis=1)
    a_off = _dot3(qcat, kcat, _dot_nt)

    a_diag = jnp.zeros((c, c), F32)
    row_in_sub = row & (HG_SUB - 1)
    for d in range(HG_SUB):
        if d == 0:
            term = q * k
        else:
            bs = pltpu.roll(b, d, axis=0)
            ks = pltpu.roll(k, d, axis=0)
            term = q * ks * jnp.exp(jnp.minimum(b - bs, 0.0))
        a_d = jnp.sum(term, axis=-1, keepdims=True)
        a_diag = jnp.where((col == row - d) & (row_in_sub >= d), a_d, a_diag)

    a = jnp.where((row >> _SUB_SHIFT) > (col >> _SUB_SHIFT), a_off, a_diag)
    o = o + _dot3(a, v, _dot)

    kdec = k * jnp.exp(b_last - b)
    st_new = st * jnp.exp(b_last) + _dot3(v, kdec, _dot_tn)
    return o, st_new


def _hgrn_kernel(q_ref, g_ref, k_ref, v_ref, ga_ref, gn_ref, o_ref, st_ref):
    @pl.when(pl.program_id(1) == 0)
    def _():
        st_ref[...] = jnp.zeros_like(st_ref)

    gn = gn_ref[...]
    for ci in range(q_ref.shape[0] // HG_CHUNK):
        rows = pl.ds(ci * HG_CHUNK, HG_CHUNK)
        o, st_new = _hgrn_chunk(q_ref[rows, :], g_ref[rows, :], k_ref[rows, :], v_ref[rows, :],
                                st_ref[...])
        st_ref[...] = st_new
        ms = jnp.mean(o * o, axis=-1, keepdims=True)
        o_ref[rows, :] = o * lax.rsqrt(ms + RMS_EPS) * gn * ga_ref[rows, :]


def _hgrn(qa, g, kk, ia, ga, gn, tb):
    s = qa.shape[0]
    blk = pl.BlockSpec((tb, HG_DK), lambda h, c: (c, h))
    return pl.pallas_call(
        _hgrn_kernel,
        grid=(HG_HEADS, s // tb),
        in_specs=[blk] * 5 + [pl.BlockSpec((1, HG_DV), lambda h, c: (0, h))],
        out_specs=blk,
        out_shape=jax.ShapeDtypeStruct((s, HG_WIDTH), F32),
        scratch_shapes=[pltpu.VMEM((HG_DV, HG_DK), F32)],
        compiler_params=_params("arbitrary", "arbitrary"),
        name="hgrn2",
    )(qa, g, kk, ia, ga, gn)


SB_FIRST_TILES = 3
SB_LOOP_TILES = 2


def _sb_head_round(qm, k_ref, v_ref, lane_mask, tri2, tiles, carry):
    ws, vs = [], []
    for rows, mask, valid in tiles:
        z = _dot_nt(qm, k_ref[rows, :])
        l1p = jnp.log1p(jnp.exp(-jnp.abs(z)))
        log_fail = -(jnp.maximum(z, 0.0) + l1p)
        log_beta = jnp.minimum(z, 0.0) - l1p
        if mask is not None:
            log_fail = jnp.where(mask, log_fail, 0.0)
        if valid is not None:
            log_fail = log_fail * valid
        fh, fl = _split2(log_fail)
        suffix = _dot(jnp.concatenate([fh, fl], axis=1), tri2)
        w = jnp.exp(log_beta + suffix + carry)
        if mask is not None:
            w = jnp.where(mask, w, 0.0)
        if valid is not None:
            w = w * valid
        ws.append(w.astype(BF16))
        v = v_ref[rows, :]
        vs.append(jnp.where(lane_mask, v, jnp.zeros_like(v)))
        carry = carry + jnp.sum(log_fail, axis=-1, keepdims=True)
    pv = _dot(jnp.concatenate(ws, axis=1), jnp.concatenate(vs, axis=0))
    return pv, carry


def _sb_kernel(q_ref, k_ref, v_ref, o_ref):
    i = pl.program_id(1)
    tq = SB_BLOCK
    nheads = LANES // SB_DH
    q = q_ref[...]
    lane = lax.broadcasted_iota(jnp.int32, (tq, LANES), 1)
    row = lax.broadcasted_iota(jnp.int32, (tq, tq), 0)
    col = lax.broadcasted_iota(jnp.int32, (tq, tq), 1)
    causal = col < row
    r2 = lax.broadcasted_iota(jnp.int32, (2 * tq, tq), 0) & (tq - 1)
    c2 = lax.broadcasted_iota(jnp.int32, (2 * tq, tq), 1)
    tri2 = jnp.where(r2 > c2, 1.0, 0.0).astype(BF16)
    lane_masks = [(lane >> _DH_SHIFT) == h for h in range(nheads)]
    qms = [jnp.where(m, q, jnp.zeros_like(q)) for m in lane_masks]

    def tiles_from(j0, count, first_mask):
        tiles = []
        for t in range(count):
            j = j0 - t
            rows = pl.ds(pl.multiple_of(jnp.maximum(j, 0) * tq, tq), tq)
            valid = None if t == 0 else (j >= 0).astype(F32)
            tiles.append((rows, first_mask if t == 0 else None, valid))
        return tiles

    zero_carry = jnp.zeros((tq, 1), F32)
    first = [_sb_head_round(qms[h], k_ref, v_ref, lane_masks[h], tri2,
                            tiles_from(i, SB_FIRST_TILES, causal), zero_carry)
             for h in range(nheads)]
    accs = tuple(pv for pv, _ in first)
    carries = tuple(c for _, c in first)

    def cmax_of(carries):
        return jnp.max(functools.reduce(jnp.maximum, carries))

    def cond(state):
        j, _, _, cmax = state
        return (j >= 0) & (cmax > SB_LOG_ZERO)

    def body(state):
        j, accs, carries, _ = state
        res = [_sb_head_round(qms[h], k_ref, v_ref, lane_masks[h], tri2,
                              tiles_from(j, SB_LOOP_TILES, None), carries[h])
               for h in range(nheads)]
        accs = tuple(a + pv for a, (pv, _) in zip(accs, res))
        carries = tuple(c for _, c in res)
        return j - SB_LOOP_TILES, accs, carries, cmax_of(carries)

    _, accs, _, _ = lax.while_loop(cond, body, (i - SB_FIRST_TILES, accs, carries, cmax_of(carries)))
    o_ref[...] = functools.reduce(jnp.add, accs)


def _sb_attention(qb, kb, vb):
    s = qb.shape[0]
    kv = pl.BlockSpec((s, LANES), lambda p, i: (0, p))
    qo = pl.BlockSpec((SB_BLOCK, LANES), lambda p, i: (i, p))
    return pl.pallas_call(
        _sb_kernel,
        grid=(SB_WIDTH // LANES, s // SB_BLOCK),
        in_specs=[qo, kv, kv],
        out_specs=pl.BlockSpec((SB_BLOCK, LANES), lambda p, i: (i, p)),
        out_shape=jax.ShapeDtypeStruct((s, SB_WIDTH), F32),
        compiler_params=_params("arbitrary", "arbitrary"),
        name="stickbreaking",
    )(qb, kb, vb)


def _merge_kernel(alpha, x_ref, oa_ref, ob_ref, sga_ref, sgb_ref, wa_ref, wb_ref, wo_ref,
                  g_ref, b_ref, o_ref):
    ya = _dot(oa_ref[...].astype(BF16), wa_ref[...])
    yb = _dot(ob_ref[...].astype(BF16), wb_ref[...])
    merged = sga_ref[...] * ya + sgb_ref[...] * yb
    y = alpha * x_ref[...] + _dot(merged.astype(BF16), wo_ref[...])
    o_ref[...] = _layer_norm(y, g_ref[...], b_ref[...])


def _merge(x, oa, ob, sga, sgb, w_a, w_b, w_out, ln_g, ln_b, alpha, ts):
    s = x.shape[0]
    row = lambda w: pl.BlockSpec((ts, w), lambda i: (i, 0))
    return pl.pallas_call(
        functools.partial(_merge_kernel, alpha),
        grid=(s // ts,),
        in_specs=[row(D_MODEL), row(HG_WIDTH), row(SB_WIDTH), row(D_MODEL), row(D_MODEL),
                  _const_spec((HG_WIDTH, D_MODEL)), _const_spec((SB_WIDTH, D_MODEL)),
                  _const_spec((D_MODEL, D_MODEL)), _const_spec((1, D_MODEL)), _const_spec((1, D_MODEL))],
        out_specs=row(D_MODEL),
        out_shape=jax.ShapeDtypeStruct((s, D_MODEL), F32),
        compiler_params=_params("arbitrary"),
        name="merge_ln1",
    )(x, oa, ob, sga, sgb, w_a, w_b, w_out, ln_g, ln_b)


FF_CHUNK = 256
CARRY_ROWS = 8


def _ffn_kernel(alpha, x_ref, p_ref, wup_ref, cw_ref, cb_ref, wdn_ref, wpe_ref, wpg_ref,
                g_ref, b_ref, o_ref, carry_ref):
    ts = x_ref.shape[0]

    @pl.when(pl.program_id(0) == 0)
    def _():
        carry_ref[...] = jnp.zeros_like(carry_ref)

    x1 = x_ref[...]
    xb = x1.astype(BF16)
    row = lax.broadcasted_iota(jnp.int32, (ts, FF_CHUNK), 0)

    def conv_up(c0):
        cols = slice(c0, c0 + FF_CHUNK)
        u = _dot(xb, wup_ref[:, cols])
        prev = carry_ref[:, cols]
        p1 = prev[CARRY_ROWS - 1:CARRY_ROWS, :]
        p2 = prev[CARRY_ROWS - 2:CARRY_ROWS - 1, :]
        u1 = jnp.where(row == 0, p1, pltpu.roll(u, 1, axis=0))
        u2 = jnp.where(row == 0, p2, jnp.where(row == 1, p1, pltpu.roll(u, 2, axis=0)))
        carry_ref[:, cols] = u[ts - CARRY_ROWS:, :]
        cw = cw_ref[:, cols]
        return cb_ref[:, cols] + cw[0:1, :] * u2 + cw[1:2, :] * u1 + cw[2:3, :] * u

    ffn = jnp.zeros((ts, D_MODEL), F32)
    for c in range(D_FF // FF_CHUNK):
        c_val = conv_up(c * FF_CHUNK)
        c_gate = conv_up(D_FF + c * FF_CHUNK)
        h = (jax.nn.gelu(c_gate) * c_val).astype(BF16)
        ffn = ffn + _dot(h, wdn_ref[c * FF_CHUNK:(c + 1) * FF_CHUNK, :])

    ple = _dot(p_ref[...].astype(BF16), wpe_ref[...]) * jax.nn.sigmoid(_dot(xb, wpg_ref[...]))
    o_ref[...] = _layer_norm(alpha * x1 + ffn + ple, g_ref[...], b_ref[...])


def _ffn(x1, p, w_up, conv_w, conv_b, w_down, w_pe, w_pg, ln_g, ln_b, alpha, ts):
    s = x1.shape[0]
    row = lambda w: pl.BlockSpec((ts, w), lambda i: (i, 0))
    return pl.pallas_call(
        functools.partial(_ffn_kernel, alpha),
        grid=(s // ts,),
        in_specs=[row(D_MODEL), row(PLE_DIM),
                  _const_spec((D_MODEL, 2 * D_FF)), _const_spec((CONV_W, 2 * D_FF)),
                  _const_spec((1, 2 * D_FF)), _const_spec((D_FF, D_MODEL)),
                  _const_spec((PLE_DIM, D_MODEL)), _const_spec((D_MODEL, D_MODEL)),
                  _const_spec((1, D_MODEL)), _const_spec((1, D_MODEL))],
        out_specs=row(D_MODEL),
        out_shape=jax.ShapeDtypeStruct((s, D_MODEL), F32),
        scratch_shapes=[pltpu.VMEM((CARRY_ROWS, 2 * D_FF), F32)],
        compiler_params=_params("arbitrary"),
        name="convffn_ln2",
    )(x1, p, w_up, conv_w, conv_b, w_down, w_pe, w_pg, ln_g, ln_b)


def _tiles(s):
    ts = 256 if s % 256 == 0 else SB_BLOCK
    return ts, ts


def kernel(x, p, lb_logits, w_in, hg_norm_g, w_a, w_b, w_out, ln1_g, ln1_b,
           w_up, conv_w, conv_b, w_down, w_pe, w_pg, ln2_g, ln2_b):
    batch, s, _ = x.shape
    depth = w_in.shape[0]
    assert s % SB_BLOCK == 0 and s % HG_CHUNK == 0
    alpha = float((2 * depth) ** 0.25)
    ts, tb = _tiles(s)
    lbl = lb_logits.astype(F32)
    outs = []
    for bi in range(batch):
        h = x[bi]
        for i in range(depth):
            (qa, g, kk, ia, ga, qb, kb, vb, sga, sgb) = _inproj(
                h, w_in[i].astype(BF16), lbl, i, ts)
            oa = _hgrn(qa, g, kk, ia, ga, hg_norm_g[i].reshape(1, HG_WIDTH), tb)
            ob = _sb_attention(qb, kb, vb)
            x1 = _merge(h, oa, ob, sga, sgb, w_a[i].astype(BF16), w_b[i].astype(BF16),
                        w_out[i].astype(BF16), ln1_g[i].reshape(1, D_MODEL),
                        ln1_b[i].reshape(1, D_MODEL), alpha, ts)
            h = _ffn(x1, p[i, bi], w_up[i].astype(BF16), conv_w[i], conv_b[i].reshape(1, 2 * D_FF),
                     w_down[i].astype(BF16), w_pe[i].astype(BF16), w_pg[i].astype(BF16),
                     ln2_g[i].reshape(1, D_MODEL), ln2_b[i].reshape(1, D_MODEL), alpha, ts)
        outs.append(h)
    return jnp.stack(outs, axis=0)
```

```python
import functools

import jax
import jax.numpy as jnp
from jax import lax
from jax.experimental import pallas as pl
from jax.experimental.pallas import tpu as pltpu

F32 = jnp.float32
BF16 = jnp.bfloat16

D_MODEL = 1024
PLE_DIM = 256
HG_HEADS = 4
HG_DK = 128
HG_DV = 128
HG_WIDTH = HG_HEADS * HG_DV
SB_HEADS = 8
SB_DH = 64
SB_WIDTH = SB_HEADS * SB_DH
D_FF = 2816
CONV_W = 3
LN_EPS = 1e-5
RMS_EPS = 1e-6
F_FLOOR = 1e-30

_OFF_QA = 0
_OFF_FA = _OFF_QA + HG_HEADS * HG_DK
_OFF_IA = _OFF_FA + HG_HEADS * HG_DK
_OFF_GA = _OFF_IA + HG_WIDTH
_OFF_QB = _OFF_GA + HG_WIDTH
_OFF_KB = _OFF_QB + SB_WIDTH
_OFF_VB = _OFF_KB + SB_WIDTH
_OFF_GATE_A = _OFF_VB + SB_WIDTH
_OFF_GATE_B = _OFF_GATE_A + D_MODEL
IN_COLS = _OFF_GATE_B + D_MODEL

HG_CHUNK = 64
HG_SUB = 8
SB_BLOCK = 128
LANES = 128
SB_LOG_ZERO = -110.0
_SUB_SHIFT = HG_SUB.bit_length() - 1
_DH_SHIFT = SB_DH.bit_length() - 1

VMEM_LIMIT = 56 * 1024 * 1024


def _params(*sem):
    return pltpu.CompilerParams(dimension_semantics=sem, vmem_limit_bytes=VMEM_LIMIT)


def _layer_spec(layer, shape):
    zeros = (0,) * len(shape)
    return pl.BlockSpec((None,) + tuple(shape), lambda *_: (layer,) + zeros,
                        pipeline_mode=pl.Buffered(1))


def _split2(a):
    hi = a.astype(BF16)
    lo = (a - hi.astype(F32)).astype(BF16)
    return hi, lo


def _dot(a, b):
    return jnp.dot(a, b, preferred_element_type=F32)


def _dot_nt(a, b):
    return lax.dot_general(a, b, (((1,), (1,)), ((), ())), preferred_element_type=F32)


def _dot_tn(a, b):
    return lax.dot_general(a, b, (((0,), (0,)), ((), ())), preferred_element_type=F32)


def _layer_norm(y, g, b):
    mu = jnp.mean(y, axis=-1, keepdims=True)
    yc = y - mu
    var = jnp.mean(yc * yc, axis=-1, keepdims=True)
    return yc * lax.rsqrt(var + LN_EPS) * g + b


def _inproj_kernel(layer, x_ref, w_ref, lbl_ref,
                   qa_ref, g_ref, kk_ref, ia_ref, ga_ref,
                   qb_ref, kb_ref, vb_ref, sga_ref, sgb_ref):
    xb = x_ref[...].astype(BF16)

    def proj(c0, width):
        return _dot(xb, w_ref[:, c0:c0 + width])

    logits = lbl_ref[...]
    m = jnp.max(logits, axis=0, keepdims=True)
    e = jnp.exp(logits - m)
    sm = e / jnp.sum(e, axis=0, keepdims=True)
    csum = sm[0:1]
    for j in range(1, layer + 1):
        csum = csum + sm[j:j + 1]
    lb = csum - sm[0:1]

    qa_ref[...] = jax.nn.silu(proj(_OFF_QA, HG_WIDTH))
    z = proj(_OFF_FA, HG_WIDTH)
    f_gate = lb + (1.0 - lb) * jax.nn.sigmoid(z)
    g_ref[...] = jnp.log(jnp.maximum(f_gate, F_FLOOR))
    kk_ref[...] = (1.0 - lb) * jax.nn.sigmoid(-z)
    ia_ref[...] = proj(_OFF_IA, HG_WIDTH)
    ga_ref[...] = jax.nn.silu(proj(_OFF_GA, HG_WIDTH))

    qb_ref[...] = (proj(_OFF_QB, SB_WIDTH) * (SB_DH ** -0.5)).astype(BF16)
    kb_ref[...] = proj(_OFF_KB, SB_WIDTH).astype(BF16)
    vb_ref[...] = proj(_OFF_VB, SB_WIDTH).astype(BF16)
    sga_ref[...] = jax.nn.sigmoid(proj(_OFF_GATE_A, D_MODEL)).astype(BF16)
    sgb_ref[...] = jax.nn.sigmoid(proj(_OFF_GATE_B, D_MODEL)).astype(BF16)


def _inproj(x, w_in, lb_logits, layer, ts):
    s = x.shape[0]
    depth = lb_logits.shape[0]
    row = lambda w: pl.BlockSpec((ts, w), lambda i: (i, 0))
    f32o = lambda w: jax.ShapeDtypeStruct((s, w), F32)
    bf16o = lambda w: jax.ShapeDtypeStruct((s, w), BF16)
    return pl.pallas_call(
        functools.partial(_inproj_kernel, layer),
        grid=(s // ts,),
        in_specs=[row(D_MODEL), _layer_spec(layer, (D_MODEL, IN_COLS)),
                  pl.BlockSpec((depth, HG_WIDTH), lambda i: (0, 0))],
        out_specs=[row(HG_WIDTH)] * 5 + [row(SB_WIDTH)] * 3 + [row(D_MODEL)] * 2,
        out_shape=[f32o(HG_WIDTH)] * 5 + [bf16o(SB_WIDTH)] * 3 + [bf16o(D_MODEL)] * 2,
        compiler_params=_params("arbitrary"),
        name="inproj",
    )(x, w_in, lb_logits)


def _hgrn_local(q, k, v, b):
    c = HG_CHUNK
    nsub = c // HG_SUB
    b_last = b[c - 1:c, :]

    zeros = lambda n: [jnp.zeros((n, HG_DK), F32)] if n else []
    g3 = lambda a: a.reshape(nsub, HG_SUB, a.shape[-1])
    b3 = g3(b)
    e_own = jnp.broadcast_to(b3[:, HG_SUB - 1:HG_SUB, :], b3.shape).reshape(c, HG_DK)
    khat = k * jnp.exp(jnp.minimum(e_own - b, 0.0))
    qparts, kparts = [], []
    for j in range(nsub - 1):
        lo = (j + 1) * HG_SUB
        e_j = b[lo - 1:lo, :]
        qparts.append(jnp.concatenate(
            zeros(lo) + [q[lo:, :] * jnp.exp(jnp.minimum(b[lo:, :] - e_j, 0.0))], axis=0))
        kparts.append(jnp.concatenate(
            zeros(lo - HG_SUB) + [khat[lo - HG_SUB:lo, :]] + zeros(c - lo), axis=0))
    a_off = _dot_nt(jnp.concatenate(qparts, axis=1).astype(BF16),
                    jnp.concatenate(kparts, axis=1).astype(BF16))

    q3, k3 = g3(q), g3(k)
    delta = (lax.broadcasted_iota(jnp.int32, (c, c), 0) - lax.broadcasted_iota(jnp.int32, (c, c), 1))
    a_diag = jnp.zeros((c, c), F32)
    for d in range(HG_SUB):
        if d == 0:
            term = q3 * k3
        else:
            bs = pltpu.roll(b3, d, axis=1)
            ks = pltpu.roll(k3, d, axis=1)
            term = q3 * ks * jnp.exp(jnp.minimum(b3 - bs, 0.0))
        a_d = jnp.sum(term.reshape(c, HG_DK), axis=-1, keepdims=True)
        a_diag = jnp.where(delta == d, a_d, a_diag)
    row_sub = lax.broadcasted_iota(jnp.int32, (c, c), 0) >> _SUB_SHIFT
    col_sub = lax.broadcasted_iota(jnp.int32, (c, c), 1) >> _SUB_SHIFT
    a = a_off + jnp.where(row_sub == col_sub, a_diag, 0.0)

    vb = v.astype(BF16)
    o_intra = _dot(a.astype(BF16), vb)
    q_dec = (q * jnp.exp(b)).astype(BF16)
    k_dec = (k * jnp.exp(b_last - b)).astype(BF16)
    return o_intra, q_dec, _dot_tn(vb, k_dec), jnp.exp(b_last)


def _hgrn_kernel(q_ref, g_ref, k_ref, v_ref, ga_ref, gn_ref, o_ref, st_ref):
    @pl.when(pl.program_id(0) == 0)
    def _():
        st_ref[...] = jnp.zeros_like(st_ref)

    c = HG_CHUNK
    nchunks = q_ref.shape[0] // c
    row = lax.broadcasted_iota(jnp.int32, (c, c), 0)
    col = lax.broadcasted_iota(jnp.int32, (c, c), 1)
    tri = jnp.where(col <= row, 1.0, 0.0).astype(BF16)

    bs = []
    for ci in range(nchunks):
        gh, gl = _split2(g_ref[pl.ds(ci * c, c), :])
        bs.append(_dot(tri, gh) + _dot(tri, gl))

    for h in range(HG_HEADS):
        cols = slice(h * HG_DK, (h + 1) * HG_DK)
        local = [_hgrn_local(q_ref[pl.ds(ci * c, c), cols], k_ref[pl.ds(ci * c, c), cols],
                             v_ref[pl.ds(ci * c, c), cols], bs[ci][:, cols])
                 for ci in range(nchunks)]
        st = st_ref[h]
        gn = gn_ref[:, cols]
        for ci, (o_intra, q_dec, st_inc, st_decay) in enumerate(local):
            o = o_intra + _dot_nt(q_dec, st.astype(BF16))
            st = st * st_decay + st_inc
            ms = jnp.mean(o * o, axis=-1, keepdims=True)
            o_ref[pl.ds(ci * c, c), cols] = (o * lax.rsqrt(ms + RMS_EPS) * gn
                                             * ga_ref[pl.ds(ci * c, c), cols])
        st_ref[h] = st


def _hgrn(qa, g, kk, ia, ga, gn, layer, tb):
    s = qa.shape[0]
    blk = pl.BlockSpec((tb, HG_WIDTH), lambda c: (c, 0))
    return pl.pallas_call(
        _hgrn_kernel,
        grid=(s // tb,),
        in_specs=[blk] * 5 + [_layer_spec(layer, (1, HG_WIDTH))],
        out_specs=blk,
        out_shape=jax.ShapeDtypeStruct((s, HG_WIDTH), F32),
        scratch_shapes=[pltpu.VMEM((HG_HEADS, HG_DV, HG_DK), F32)],
        compiler_params=_params("arbitrary"),
        name="hgrn2",
    )(qa, g, kk, ia, ga, gn)


SB_QBLOCKS = 4
SB_FIRST_TILES = 3
SB_LOOP_TILES = 2


def _sb_round(qms, lane_masks, k_ref, v_ref, tri2, tiles, carries):
    tq = SB_BLOCK
    log_fail, log_beta = [], []
    for qm, chain_tiles in zip(qms, tiles):
        for rows, mask, valid in chain_tiles:
            z = _dot_nt(qm, k_ref[rows, :])
            lg = jnp.log(1.0 + jnp.exp(-jnp.abs(z)))
            lf = jnp.minimum(-z, 0.0) - lg
            log_beta.append(lf + z)
            if mask is not None:
                lf = jnp.where(mask, lf, 0.0)
            if valid is not None:
                lf = lf * valid
            log_fail.append(lf)
    parts = [jnp.concatenate(_split2(lf), axis=1) for lf in log_fail]
    suffix = _dot(jnp.concatenate(parts, axis=0), tri2)

    out = []
    idx = 0
    for n, chain_tiles in enumerate(tiles):
        carry = carries[n]
        ws, vs = [], []
        for rows, mask, valid in chain_tiles:
            w = jnp.exp(log_beta[idx] + suffix[idx * tq:(idx + 1) * tq, :] + carry)
            if mask is not None:
                w = jnp.where(mask, w, 0.0)
            if valid is not None:
                w = w * valid
            ws.append(w.astype(BF16))
            v = v_ref[rows, :]
            vs.append(jnp.where(lane_masks[n], v, jnp.zeros_like(v)))
            carry = carry + jnp.sum(log_fail[idx], axis=-1, keepdims=True)
            idx += 1
        out.append((_dot(jnp.concatenate(ws, axis=1), jnp.concatenate(vs, axis=0)), carry))
    return out


def _sb_kernel(q_ref, k_ref, v_ref, o_ref):
    tq = SB_BLOCK
    nheads = LANES // SB_DH
    nq = q_ref.shape[0] // tq
    first_block = pl.program_id(1) * nq
    lane = lax.broadcasted_iota(jnp.int32, (tq, LANES), 1)
    row = lax.broadcasted_iota(jnp.int32, (tq, tq), 0)
    col = lax.broadcasted_iota(jnp.int32, (tq, tq), 1)
    causal = col < row
    r2 = lax.broadcasted_iota(jnp.int32, (2 * tq, tq), 0) & (tq - 1)
    c2 = lax.broadcasted_iota(jnp.int32, (2 * tq, tq), 1)
    tri2 = jnp.where(r2 > c2, 1.0, 0.0).astype(BF16)
    lane_masks = [(lane >> _DH_SHIFT) == h for h in range(nheads)]
    chains = [(a, h) for a in range(nq) for h in range(nheads)]
    qms = []
    for a, h in chains:
        q = q_ref[a * tq:(a + 1) * tq, :]
        qms.append(jnp.where(lane_masks[h], q, jnp.zeros_like(q)))

    def tiles_from(j0, count, first_mask):
        tiles = []
        for t in range(count):
            j = j0 - t
            rows = pl.ds(pl.multiple_of(jnp.maximum(j, 0) * tq, tq), tq)
            diagonal = t == 0 and first_mask is not None
            valid = None if diagonal else (j >= 0).astype(F32)
            tiles.append((rows, first_mask if diagonal else None, valid))
        return tiles

    def sweep(offset, count, first_mask, carries):
        return _sb_round(qms, [lane_masks[h] for _, h in chains], k_ref, v_ref, tri2,
                         [tiles_from(first_block + a - offset, count, first_mask) for a, _ in chains],
                         carries)

    def cmax_of(carries):
        return jnp.max(functools.reduce(jnp.maximum, carries))

    res = sweep(0, SB_FIRST_TILES, causal, [jnp.zeros((tq, 1), F32)] * len(chains))
    accs = tuple(pv for pv, _ in res)
    carries = tuple(c for _, c in res)

    def cond(state):
        offset, _, _, cmax = state
        return (first_block + (nq - 1) - offset >= 0) & (cmax > SB_LOG_ZERO)

    def body(state):
        offset, accs, carries, _ = state
        res = sweep(offset, SB_LOOP_TILES, None, carries)
        accs = tuple(acc + pv for acc, (pv, _) in zip(accs, res))
        carries = tuple(c for _, c in res)
        return offset + SB_LOOP_TILES, accs, carries, cmax_of(carries)

    _, accs, _, _ = lax.while_loop(
        cond, body, (jnp.int32(SB_FIRST_TILES), accs, carries, cmax_of(carries)))
    for a in range(nq):
        o_ref[a * tq:(a + 1) * tq, :] = functools.reduce(
            jnp.add, [accs[n] for n, (qa, _) in enumerate(chains) if qa == a])


def _sb_attention(qb, kb, vb):
    s = qb.shape[0]
    nq = SB_QBLOCKS if s % (SB_QBLOCKS * SB_BLOCK) == 0 else 1
    kv = pl.BlockSpec((s, LANES), lambda p, i: (0, p))
    qo = pl.BlockSpec((nq * SB_BLOCK, LANES), lambda p, i: (i, p))
    return pl.pallas_call(
        _sb_kernel,
        grid=(SB_WIDTH // LANES, s // (nq * SB_BLOCK)),
        in_specs=[qo, kv, kv],
        out_specs=qo,
        out_shape=jax.ShapeDtypeStruct((s, SB_WIDTH), F32),
        compiler_params=_params("arbitrary", "arbitrary"),
        name="stickbreaking",
    )(qb, kb, vb)


def _merge_kernel(alpha, x_ref, oa_ref, ob_ref, sga_ref, sgb_ref, wa_ref, wb_ref, wo_ref,
                  g_ref, b_ref, o_ref):
    ya = _dot(oa_ref[...].astype(BF16), wa_ref[...])
    yb = _dot(ob_ref[...].astype(BF16), wb_ref[...])
    merged = sga_ref[...].astype(F32) * ya + sgb_ref[...].astype(F32) * yb
    y = alpha * x_ref[...] + _dot(merged.astype(BF16), wo_ref[...])
    o_ref[...] = _layer_norm(y, g_ref[...], b_ref[...])


def _merge(x, oa, ob, sga, sgb, w_a, w_b, w_out, ln_g, ln_b, alpha, layer, ts):
    s = x.shape[0]
    row = lambda w: pl.BlockSpec((ts, w), lambda i: (i, 0))
    return pl.pallas_call(
        functools.partial(_merge_kernel, alpha),
        grid=(s // ts,),
        in_specs=[row(D_MODEL), row(HG_WIDTH), row(SB_WIDTH), row(D_MODEL), row(D_MODEL),
                  _layer_spec(layer, (HG_WIDTH, D_MODEL)), _layer_spec(layer, (SB_WIDTH, D_MODEL)),
                  _layer_spec(layer, (D_MODEL, D_MODEL)), _layer_spec(layer, (1, D_MODEL)),
                  _layer_spec(layer, (1, D_MODEL))],
        out_specs=row(D_MODEL),
        out_shape=jax.ShapeDtypeStruct((s, D_MODEL), F32),
        compiler_params=_params("arbitrary"),
        name="merge_ln1",
    )(x, oa, ob, sga, sgb, w_a, w_b, w_out, ln_g, ln_b)


FF_CHUNK = 256
CARRY_ROWS = 8


def _ffn_kernel(alpha, x_ref, p_ref, wup_ref, cw_ref, cb_ref, wdn_ref, wpe_ref, wpg_ref,
                g_ref, b_ref, o_ref, carry_ref):
    ts = x_ref.shape[0]

    @pl.when(pl.program_id(0) == 0)
    def _():
        carry_ref[...] = jnp.zeros_like(carry_ref)

    x1 = x_ref[...]
    xb = x1.astype(BF16)
    row = lax.broadcasted_iota(jnp.int32, (ts, FF_CHUNK), 0)

    def conv_up(c0):
        cols = slice(c0, c0 + FF_CHUNK)
        u = _dot(xb, wup_ref[:, cols])
        prev = carry_ref[:, cols]
        p1 = prev[CARRY_ROWS - 1:CARRY_ROWS, :]
        p2 = prev[CARRY_ROWS - 2:CARRY_ROWS - 1, :]
        u1 = jnp.where(row == 0, p1, pltpu.roll(u, 1, axis=0))
        u2 = jnp.where(row == 0, p2, jnp.where(row == 1, p1, pltpu.roll(u, 2, axis=0)))
        carry_ref[:, cols] = u[ts - CARRY_ROWS:, :]
        cw = cw_ref[:, cols]
        return cb_ref[:, cols] + cw[0:1, :] * u2 + cw[1:2, :] * u1 + cw[2:3, :] * u

    nchunks = D_FF // FF_CHUNK
    hs = []
    for c in range(nchunks):
        c_val = conv_up(c * FF_CHUNK)
        c_gate = conv_up(D_FF + c * FF_CHUNK)
        hs.append((jax.nn.gelu(c_gate) * c_val).astype(BF16))
    ffn = _dot(jnp.concatenate(hs, axis=1), wdn_ref[...])

    ple = _dot(p_ref[...].astype(BF16), wpe_ref[...]) * jax.nn.sigmoid(_dot(xb, wpg_ref[...]))
    o_ref[...] = _layer_norm(alpha * x1 + ffn + ple, g_ref[...], b_ref[...])


def _ffn(x1, p, bi, w_up, conv_w, conv_b, w_down, w_pe, w_pg, ln_g, ln_b, alpha, layer, ts):
    s = x1.shape[0]
    row = lambda w: pl.BlockSpec((ts, w), lambda i: (i, 0))
    return pl.pallas_call(
        functools.partial(_ffn_kernel, alpha),
        grid=(s // ts,),
        in_specs=[row(D_MODEL), pl.BlockSpec((None, None, ts, PLE_DIM), lambda i: (layer, bi, i, 0)),
                  _layer_spec(layer, (D_MODEL, 2 * D_FF)), _layer_spec(layer, (CONV_W, 2 * D_FF)),
                  _layer_spec(layer, (1, 2 * D_FF)), _layer_spec(layer, (D_FF, D_MODEL)),
                  _layer_spec(layer, (PLE_DIM, D_MODEL)), _layer_spec(layer, (D_MODEL, D_MODEL)),
                  _layer_spec(layer, (1, D_MODEL)), _layer_spec(layer, (1, D_MODEL))],
        out_specs=row(D_MODEL),
        out_shape=jax.ShapeDtypeStruct((s, D_MODEL), F32),
        scratch_shapes=[pltpu.VMEM((CARRY_ROWS, 2 * D_FF), F32)],
        compiler_params=_params("arbitrary"),
        name="convffn_ln2",
    )(x1, p, w_up, conv_w, conv_b, w_down, w_pe, w_pg, ln_g, ln_b)


def _tiles(s):
    ts = next(t for t in (512, 256, SB_BLOCK) if s % t == 0)
    return ts, min(ts, 256)


def kernel(x, p, lb_logits, w_in, hg_norm_g, w_a, w_b, w_out, ln1_g, ln1_b,
           w_up, conv_w, conv_b, w_down, w_pe, w_pg, ln2_g, ln2_b):
    batch, s, _ = x.shape
    depth = w_in.shape[0]
    assert s % SB_BLOCK == 0 and s % HG_CHUNK == 0
    alpha = float((2 * depth) ** 0.25)
    ts, tb = _tiles(s)
    lbl = lb_logits.astype(F32)
    vec = lambda a: a.reshape(depth, 1, a.shape[-1])
    w_in, w_a, w_b, w_out, w_up, w_down, w_pe, w_pg = (
        w.astype(BF16) for w in (w_in, w_a, w_b, w_out, w_up, w_down, w_pe, w_pg))
    outs = []
    for bi in range(batch):
        h = x[bi]
        for i in range(depth):
            (qa, g, kk, ia, ga, qb, kb, vb, sga, sgb) = _inproj(h, w_in, lbl, i, ts)
            oa = _hgrn(qa, g, kk, ia, ga, vec(hg_norm_g), i, tb)
            ob = _sb_attention(qb, kb, vb)
            x1 = _merge(h, oa, ob, sga, sgb, w_a, w_b, w_out, vec(ln1_g), vec(ln1_b), alpha, i, ts)
            h = _ffn(x1, p, bi, w_up, conv_w, vec(conv_b), w_down, w_pe, w_pg,
                     vec(ln2_g), vec(ln2_b), alpha, i, ts)
        outs.append(h)
    return jnp.stack(outs, axis=0)
```

```python
import functools

import jax
import jax.numpy as jnp
from jax import lax
from jax.experimental import pallas as pl
from jax.experimental.pallas import tpu as pltpu

F32 = jnp.float32
BF16 = jnp.bfloat16

D_MODEL = 1024
PLE_DIM = 256
HG_HEADS = 4
HG_DK = 128
HG_DV = 128
HG_WIDTH = HG_HEADS * HG_DV
SB_HEADS = 8
SB_DH = 64
SB_WIDTH = SB_HEADS * SB_DH
D_FF = 2816
CONV_W = 3
LN_EPS = 1e-5
RMS_EPS = 1e-6
F_FLOOR = 1e-30

_OFF_QA = 0
_OFF_FA = _OFF_QA + HG_HEADS * HG_DK
_OFF_IA = _OFF_FA + HG_HEADS * HG_DK
_OFF_GA = _OFF_IA + HG_WIDTH
_OFF_QB = _OFF_GA + HG_WIDTH
_OFF_KB = _OFF_QB + SB_WIDTH
_OFF_VB = _OFF_KB + SB_WIDTH
_OFF_GATE_A = _OFF_VB + SB_WIDTH
_OFF_GATE_B = _OFF_GATE_A + D_MODEL
IN_COLS = _OFF_GATE_B + D_MODEL

HG_CHUNK = 64
HG_SUB = 8
SB_BLOCK = 128
LANES = 128
SB_LOG_ZERO = -110.0
_SUB_SHIFT = HG_SUB.bit_length() - 1
_DH_SHIFT = SB_DH.bit_length() - 1

VMEM_LIMIT = 56 * 1024 * 1024


def _params(*sem):
    return pltpu.CompilerParams(dimension_semantics=sem, vmem_limit_bytes=VMEM_LIMIT)


def _layer_spec(layer, shape):
    zeros = (0,) * len(shape)
    return pl.BlockSpec((None,) + tuple(shape), lambda *_: (layer,) + zeros,
                        pipeline_mode=pl.Buffered(1))


def _split2(a):
    hi = a.astype(BF16)
    lo = (a - hi.astype(F32)).astype(BF16)
    return hi, lo


def _dot(a, b):
    return jnp.dot(a, b, preferred_element_type=F32)


def _dot_nt(a, b):
    return lax.dot_general(a, b, (((1,), (1,)), ((), ())), preferred_element_type=F32)


def _dot_tn(a, b):
    return lax.dot_general(a, b, (((0,), (0,)), ((), ())), preferred_element_type=F32)


def _layer_norm(y, g, b):
    mu = jnp.mean(y, axis=-1, keepdims=True)
    yc = y - mu
    var = jnp.mean(yc * yc, axis=-1, keepdims=True)
    return yc * lax.rsqrt(var + LN_EPS) * g + b


def _hgrn_local(q, k, v, b):
    c = HG_CHUNK
    nsub = c // HG_SUB
    b_last = b[c - 1:c, :]

    zeros = lambda n: [jnp.zeros((n, HG_DK), F32)] if n else []
    g3 = lambda a: a.reshape(nsub, HG_SUB, a.shape[-1])
    b3 = g3(b)
    e_own = jnp.broadcast_to(b3[:, HG_SUB - 1:HG_SUB, :], b3.shape).reshape(c, HG_DK)
    khat = k * jnp.exp(jnp.minimum(e_own - b, 0.0))
    qparts, kparts = [], []
    for j in range(nsub - 1):
        lo = (j + 1) * HG_SUB
        e_j = b[lo - 1:lo, :]
        qparts.append(jnp.concatenate(
            zeros(lo) + [q[lo:, :] * jnp.exp(jnp.minimum(b[lo:, :] - e_j, 0.0))], axis=0))
        kparts.append(jnp.concatenate(
            zeros(lo - HG_SUB) + [khat[lo - HG_SUB:lo, :]] + zeros(c - lo), axis=0))
    a_off = _dot_nt(jnp.concatenate(qparts, axis=1).astype(BF16),
                    jnp.concatenate(kparts, axis=1).astype(BF16))

    q3, k3 = g3(q), g3(k)
    delta = (lax.broadcasted_iota(jnp.int32, (c, c), 0) - lax.broadcasted_iota(jnp.int32, (c, c), 1))
    a_diag = jnp.zeros((c, c), F32)
    for d in range(HG_SUB):
        if d == 0:
            term = q3 * k3
        else:
            bs = pltpu.roll(b3, d, axis=1)
            ks = pltpu.roll(k3, d, axis=1)
            term = q3 * ks * jnp.exp(jnp.minimum(b3 - bs, 0.0))
        a_d = jnp.sum(term.reshape(c, HG_DK), axis=-1, keepdims=True)
        a_diag = jnp.where(delta == d, a_d, a_diag)
    row_sub = lax.broadcasted_iota(jnp.int32, (c, c), 0) >> _SUB_SHIFT
    col_sub = lax.broadcasted_iota(jnp.int32, (c, c), 1) >> _SUB_SHIFT
    a = a_off + jnp.where(row_sub == col_sub, a_diag, 0.0)

    vb = v.astype(BF16)
    o_intra = _dot(a.astype(BF16), vb)
    q_dec = (q * jnp.exp(b)).astype(BF16)
    k_dec = (k * jnp.exp(b_last - b)).astype(BF16)
    return o_intra, q_dec, _dot_tn(vb, k_dec), jnp.exp(b_last)


def _hgrn_rows(q, g, k, v, ga, gn, st_ref, o_ref, after_head):
    c = HG_CHUNK
    nchunks = q.shape[0] // c
    rows = [slice(ci * c, (ci + 1) * c) for ci in range(nchunks)]
    row = lax.broadcasted_iota(jnp.int32, (c, c), 0)
    col = lax.broadcasted_iota(jnp.int32, (c, c), 1)
    tri = jnp.where(col <= row, 1.0, 0.0).astype(BF16)

    bs = []
    for r in rows:
        gh, gl = _split2(g[r, :])
        bs.append(_dot(tri, gh) + _dot(tri, gl))

    for h in range(HG_HEADS):
        cols = slice(h * HG_DK, (h + 1) * HG_DK)
        local = [_hgrn_local(q[r, cols], k[r, cols], v[r, cols], b[:, cols]) for r, b in zip(rows, bs)]
        st = st_ref[h]
        for r, (o_intra, q_dec, st_inc, st_decay) in zip(rows, local):
            o = o_intra + _dot_nt(q_dec, st.astype(BF16))
            st = st * st_decay + st_inc
            ms = jnp.mean(o * o, axis=-1, keepdims=True)
            o_ref[r, cols] = o * lax.rsqrt(ms + RMS_EPS) * gn[:, cols] * ga[r, cols]
        st_ref[h] = st
        after_head[h]()


def _inproj_kernel(layer, x_ref, w_ref, lbl_ref, gn_ref,
                   oa_ref, qb_ref, kb_ref, vb_ref, sga_ref, sgb_ref, st_ref):
    @pl.when(pl.program_id(0) == 0)
    def _():
        st_ref[...] = jnp.zeros_like(st_ref)

    xb = x_ref[...].astype(BF16)

    def proj(c0, width):
        return _dot(xb, w_ref[:, c0:c0 + width])

    logits = lbl_ref[...]
    m = jnp.max(logits, axis=0, keepdims=True)
    e = jnp.exp(logits - m)
    sm = e / jnp.sum(e, axis=0, keepdims=True)
    csum = sm[0:1]
    for j in range(1, layer + 1):
        csum = csum + sm[j:j + 1]
    lb = csum - sm[0:1]

    def attention_piece(ref, c0, scale):
        def emit():
            ref[...] = (proj(c0, SB_WIDTH) * scale).astype(BF16)
        return emit

    def gate_piece(ref, c0, j):
        def emit():
            ref[:, j:j + HG_WIDTH] = jax.nn.sigmoid(proj(c0 + j, HG_WIDTH)).astype(BF16)
        return emit

    pieces = [attention_piece(qb_ref, _OFF_QB, SB_DH ** -0.5), attention_piece(kb_ref, _OFF_KB, 1.0),
              attention_piece(vb_ref, _OFF_VB, 1.0)]
    pieces += [gate_piece(sga_ref, _OFF_GATE_A, j) for j in range(0, D_MODEL, HG_WIDTH)]
    pieces += [gate_piece(sgb_ref, _OFF_GATE_B, j) for j in range(0, D_MODEL, HG_WIDTH)]
    per_head = -(-len(pieces) // HG_HEADS)

    def rest_of_projection(h):
        def emit():
            for piece in pieces[h * per_head:(h + 1) * per_head]:
                piece()
        return emit

    z = proj(_OFF_FA, HG_WIDTH)
    f_gate = lb + (1.0 - lb) * jax.nn.sigmoid(z)
    _hgrn_rows(jax.nn.silu(proj(_OFF_QA, HG_WIDTH)),
               jnp.log(jnp.maximum(f_gate, F_FLOOR)),
               (1.0 - lb) * jax.nn.sigmoid(-z),
               proj(_OFF_IA, HG_WIDTH),
               jax.nn.silu(proj(_OFF_GA, HG_WIDTH)),
               gn_ref[...], st_ref, oa_ref, [rest_of_projection(h) for h in range(HG_HEADS)])


def _inproj_hgrn(x, w_in, lb_logits, gn, layer, ts):
    s = x.shape[0]
    depth = lb_logits.shape[0]
    row = lambda w: pl.BlockSpec((ts, w), lambda i: (i, 0))
    bf16o = lambda w: jax.ShapeDtypeStruct((s, w), BF16)
    return pl.pallas_call(
        functools.partial(_inproj_kernel, layer),
        grid=(s // ts,),
        in_specs=[row(D_MODEL), _layer_spec(layer, (D_MODEL, IN_COLS)),
                  pl.BlockSpec((depth, HG_WIDTH), lambda i: (0, 0)),
                  _layer_spec(layer, (1, HG_WIDTH))],
        out_specs=[row(HG_WIDTH)] + [row(SB_WIDTH)] * 3 + [row(D_MODEL)] * 2,
        out_shape=[jax.ShapeDtypeStruct((s, HG_WIDTH), F32)] + [bf16o(SB_WIDTH)] * 3
        + [bf16o(D_MODEL)] * 2,
        scratch_shapes=[pltpu.VMEM((HG_HEADS, HG_DV, HG_DK), F32)],
        compiler_params=_params("arbitrary"),
        name="inproj_hgrn2",
    )(x, w_in, lb_logits, gn)


SB_QBLOCKS = 8
SB_FIRST_TILES = 3
SB_LOOP_TILES = 2
SB_GROUP = 8


def _sb_round(qms, lane_masks, k_ref, v_ref, tri2, tiles, carries):
    tq = SB_BLOCK
    log_fail, log_beta = [], []
    for qm, chain_tiles in zip(qms, tiles):
        for rows, mask, valid in chain_tiles:
            z = _dot_nt(qm, k_ref[rows, :])
            lg = jnp.log(1.0 + jnp.exp(-jnp.abs(z)))
            lf = jnp.minimum(-z, 0.0) - lg
            log_beta.append(lf + z)
            if mask is not None:
                lf = jnp.where(mask, lf, 0.0)
            if valid is not None:
                lf = lf * valid
            log_fail.append(lf)
    parts = [jnp.concatenate(_split2(lf), axis=1) for lf in log_fail]
    suffix = _dot(jnp.concatenate(parts, axis=0), tri2)

    out = []
    idx = 0
    for n, chain_tiles in enumerate(tiles):
        carry = carries[n]
        ws, vs = [], []
        for rows, mask, valid in chain_tiles:
            w = jnp.exp(log_beta[idx] + suffix[idx * tq:(idx + 1) * tq, :] + carry)
            if mask is not None:
                w = jnp.where(mask, w, 0.0)
            if valid is not None:
                w = w * valid
            ws.append(w.astype(BF16))
            v = v_ref[rows, :]
            vs.append(jnp.where(lane_masks[n], v, jnp.zeros_like(v)))
            carry = carry + jnp.sum(log_fail[idx], axis=-1, keepdims=True)
            idx += 1
        out.append((_dot(jnp.concatenate(ws, axis=1), jnp.concatenate(vs, axis=0)), carry))
    return out


def _sb_kernel(q_ref, k_ref, v_ref, o_ref):
    tq = SB_BLOCK
    nheads = LANES // SB_DH
    nq = q_ref.shape[0] // tq
    first_block = pl.program_id(1) * nq
    lane = lax.broadcasted_iota(jnp.int32, (tq, LANES), 1)
    row = lax.broadcasted_iota(jnp.int32, (tq, tq), 0)
    col = lax.broadcasted_iota(jnp.int32, (tq, tq), 1)
    causal = col < row
    r2 = lax.broadcasted_iota(jnp.int32, (2 * tq, tq), 0) & (tq - 1)
    c2 = lax.broadcasted_iota(jnp.int32, (2 * tq, tq), 1)
    tri2 = jnp.where(r2 > c2, 1.0, 0.0).astype(BF16)
    lane_masks = [(lane >> _DH_SHIFT) == h for h in range(nheads)]
    chains = [(a, h) for a in range(nq) for h in range(nheads)]
    qms = []
    for a, h in chains:
        q = q_ref[a * tq:(a + 1) * tq, :]
        qms.append(jnp.where(lane_masks[h], q, jnp.zeros_like(q)))

    def tiles_from(j0, count, first_mask):
        tiles = []
        for t in range(count):
            j = j0 - t
            rows = pl.ds(pl.multiple_of(jnp.maximum(j, 0) * tq, tq), tq)
            diagonal = t == 0 and first_mask is not None
            valid = None if diagonal else (j >= 0).astype(F32)
            tiles.append((rows, first_mask if diagonal else None, valid))
        return tiles

    def sweep(offset, count, first_mask, carries):
        res = []
        for g in range(0, len(chains), SB_GROUP):
            grp = range(g, min(g + SB_GROUP, len(chains)))
            res += _sb_round([qms[n] for n in grp], [lane_masks[chains[n][1]] for n in grp],
                             k_ref, v_ref, tri2,
                             [tiles_from(first_block + chains[n][0] - offset, count, first_mask)
                              for n in grp],
                             [carries[n] for n in grp])
        return res

    def cmax_of(carries):
        return jnp.max(functools.reduce(jnp.maximum, carries))

    res = sweep(0, SB_FIRST_TILES, causal, [jnp.zeros((tq, 1), F32)] * len(chains))
    accs = tuple(pv for pv, _ in res)
    carries = tuple(c for _, c in res)

    def cond(state):
        offset, _, _, cmax = state
        return (first_block + (nq - 1) - offset >= 0) & (cmax > SB_LOG_ZERO)

    def body(state):
        offset, accs, carries, _ = state
        res = sweep(offset, SB_LOOP_TILES, None, carries)
        accs = tuple(acc + pv for acc, (pv, _) in zip(accs, res))
        carries = tuple(c for _, c in res)
        return offset + SB_LOOP_TILES, accs, carries, cmax_of(carries)

    _, accs, _, _ = lax.while_loop(
        cond, body, (jnp.int32(SB_FIRST_TILES), accs, carries, cmax_of(carries)))
    for a in range(nq):
        o_ref[a * tq:(a + 1) * tq, :] = functools.reduce(
            jnp.add, [accs[n] for n, (qa, _) in enumerate(chains) if qa == a])


def _sb_attention(qb, kb, vb):
    s = qb.shape[0]
    nq = SB_QBLOCKS if s % (SB_QBLOCKS * SB_BLOCK) == 0 else 1
    kv = pl.BlockSpec((s, LANES), lambda p, i: (0, p))
    qo = pl.BlockSpec((nq * SB_BLOCK, LANES), lambda p, i: (i, p))
    return pl.pallas_call(
        _sb_kernel,
        grid=(SB_WIDTH // LANES, s // (nq * SB_BLOCK)),
        in_specs=[qo, kv, kv],
        out_specs=qo,
        out_shape=jax.ShapeDtypeStruct((s, SB_WIDTH), F32),
        compiler_params=_params("arbitrary", "arbitrary"),
        name="stickbreaking",
    )(qb, kb, vb)


def _merge_kernel(alpha, x_ref, oa_ref, ob_ref, sga_ref, sgb_ref, wa_ref, wb_ref, wo_ref,
                  g_ref, b_ref, o_ref):
    ya = _dot(oa_ref[...].astype(BF16), wa_ref[...])
    yb = _dot(ob_ref[...].astype(BF16), wb_ref[...])
    merged = sga_ref[...].astype(F32) * ya + sgb_ref[...].astype(F32) * yb
    y = alpha * x_ref[...] + _dot(merged.astype(BF16), wo_ref[...])
    o_ref[...] = _layer_norm(y, g_ref[...], b_ref[...])


def _merge(x, oa, ob, sga, sgb, w_a, w_b, w_out, ln_g, ln_b, alpha, layer, ts):
    s = x.shape[0]
    row = lambda w: pl.BlockSpec((ts, w), lambda i: (i, 0))
    return pl.pallas_call(
        functools.partial(_merge_kernel, alpha),
        grid=(s // ts,),
        in_specs=[row(D_MODEL), row(HG_WIDTH), row(SB_WIDTH), row(D_MODEL), row(D_MODEL),
                  _layer_spec(layer, (HG_WIDTH, D_MODEL)), _layer_spec(layer, (SB_WIDTH, D_MODEL)),
                  _layer_spec(layer, (D_MODEL, D_MODEL)), _layer_spec(layer, (1, D_MODEL)),
                  _layer_spec(layer, (1, D_MODEL))],
        out_specs=row(D_MODEL),
        out_shape=jax.ShapeDtypeStruct((s, D_MODEL), F32),
        compiler_params=_params("arbitrary"),
        name="merge_ln1",
    )(x, oa, ob, sga, sgb, w_a, w_b, w_out, ln_g, ln_b)


FF_CHUNK = 256
CARRY_ROWS = 8


def _ffn_kernel(alpha, x_ref, p_ref, wup_ref, cw_ref, cb_ref, wdn_ref, wpe_ref, wpg_ref,
                g_ref, b_ref, o_ref, carry_ref):
    ts = x_ref.shape[0]

    @pl.when(pl.program_id(0) == 0)
    def _():
        carry_ref[...] = jnp.zeros_like(carry_ref)

    x1 = x_ref[...]
    xb = x1.astype(BF16)
    row = lax.broadcasted_iota(jnp.int32, (ts, FF_CHUNK), 0)

    def up(c0):
        return _dot(xb, wup_ref[:, c0:c0 + FF_CHUNK])

    def conv(u, c0):
        cols = slice(c0, c0 + FF_CHUNK)
        prev = carry_ref[:, cols]
        p1 = prev[CARRY_ROWS - 1:CARRY_ROWS, :]
        p2 = prev[CARRY_ROWS - 2:CARRY_ROWS - 1, :]
        u1 = jnp.where(row == 0, p1, pltpu.roll(u, 1, axis=0))
        u2 = jnp.where(row == 0, p2, jnp.where(row == 1, p1, pltpu.roll(u, 2, axis=0)))
        carry_ref[:, cols] = u[ts - CARRY_ROWS:, :]
        cw = cw_ref[:, cols]
        return cb_ref[:, cols] + cw[0:1, :] * u2 + cw[1:2, :] * u1 + cw[2:3, :] * u

    nchunks = D_FF // FF_CHUNK
    hs = []
    for c in range(nchunks):
        c_val = conv(up(c * FF_CHUNK), c * FF_CHUNK)
        c_gate = conv(up(D_FF + c * FF_CHUNK), D_FF + c * FF_CHUNK)
        hs.append((jax.nn.gelu(c_gate) * c_val).astype(BF16))
    ffn = _dot(jnp.concatenate(hs, axis=1), wdn_ref[...])

    ple = _dot(p_ref[...].astype(BF16), wpe_ref[...]) * jax.nn.sigmoid(_dot(xb, wpg_ref[...]))
    o_ref[...] = _layer_norm(alpha * x1 + ffn + ple, g_ref[...], b_ref[...])


def _ffn(x1, p, bi, w_up, conv_w, conv_b, w_down, w_pe, w_pg, ln_g, ln_b, alpha, layer, ts):
    s = x1.shape[0]
    row = lambda w: pl.BlockSpec((ts, w), lambda i: (i, 0))
    return pl.pallas_call(
        functools.partial(_ffn_kernel, alpha),
        grid=(s // ts,),
        in_specs=[row(D_MODEL), pl.BlockSpec((None, None, ts, PLE_DIM), lambda i: (layer, bi, i, 0)),
                  _layer_spec(layer, (D_MODEL, 2 * D_FF)), _layer_spec(layer, (CONV_W, 2 * D_FF)),
                  _layer_spec(layer, (1, 2 * D_FF)), _layer_spec(layer, (D_FF, D_MODEL)),
                  _layer_spec(layer, (PLE_DIM, D_MODEL)), _layer_spec(layer, (D_MODEL, D_MODEL)),
                  _layer_spec(layer, (1, D_MODEL)), _layer_spec(layer, (1, D_MODEL))],
        out_specs=row(D_MODEL),
        out_shape=jax.ShapeDtypeStruct((s, D_MODEL), F32),
        scratch_shapes=[pltpu.VMEM((CARRY_ROWS, 2 * D_FF), F32)],
        compiler_params=_params("arbitrary"),
        name="convffn_ln2",
    )(x1, p, w_up, conv_w, conv_b, w_down, w_pe, w_pg, ln_g, ln_b)


def _tiles(s):
    return next(t for t in (512, 256, SB_BLOCK) if s % t == 0)


def kernel(x, p, lb_logits, w_in, hg_norm_g, w_a, w_b, w_out, ln1_g, ln1_b,
           w_up, conv_w, conv_b, w_down, w_pe, w_pg, ln2_g, ln2_b):
    batch, s, _ = x.shape
    depth = w_in.shape[0]
    assert s % SB_BLOCK == 0 and s % HG_CHUNK == 0
    alpha = float((2 * depth) ** 0.25)
    ts = _tiles(s)
    lbl = lb_logits.astype(F32)
    vec = lambda a: a.reshape(depth, 1, a.shape[-1])
    w_in, w_a, w_b, w_out, w_up, w_down, w_pe, w_pg = (
        w.astype(BF16) for w in (w_in, w_a, w_b, w_out, w_up, w_down, w_pe, w_pg))
    outs = []
    for bi in range(batch):
        h = x[bi]
        for i in range(depth):
            oa, qb, kb, vb, sga, sgb = _inproj_hgrn(h, w_in, lbl, vec(hg_norm_g), i, ts)
            ob = _sb_attention(qb, kb, vb)
            x1 = _merge(h, oa, ob, sga, sgb, w_a, w_b, w_out, vec(ln1_g), vec(ln1_b), alpha, i, ts)
            h = _ffn(x1, p, bi, w_up, conv_w, vec(conv_b), w_down, w_pe, w_pg,
                     vec(ln2_g), vec(ln2_b), alpha, i, ts)
        outs.append(h)
    return jnp.stack(outs, axis=0)
```

```python
import functools

import jax
import jax.numpy as jnp
from jax import lax
from jax.experimental import pallas as pl
from jax.experimental.pallas import tpu as pltpu

F32 = jnp.float32
BF16 = jnp.bfloat16

D_MODEL = 1024
PLE_DIM = 256
HG_HEADS = 4
HG_DK = 128
HG_DV = 128
HG_WIDTH = HG_HEADS * HG_DV
SB_HEADS = 8
SB_DH = 64
SB_WIDTH = SB_HEADS * SB_DH
D_FF = 2816
CONV_W = 3
LN_EPS = 1e-5
RMS_EPS = 1e-6
F_FLOOR = 1e-30

_OFF_QA = 0
_OFF_FA = _OFF_QA + HG_HEADS * HG_DK
_OFF_IA = _OFF_FA + HG_HEADS * HG_DK
_OFF_GA = _OFF_IA + HG_WIDTH
_OFF_QB = _OFF_GA + HG_WIDTH
_OFF_KB = _OFF_QB + SB_WIDTH
_OFF_VB = _OFF_KB + SB_WIDTH
_OFF_GATE_A = _OFF_VB + SB_WIDTH
_OFF_GATE_B = _OFF_GATE_A + D_MODEL
IN_COLS = _OFF_GATE_B + D_MODEL

HG_CHUNK = 64
HG_SUB = 8
HG_SUBTILES = 1
SB_BLOCK = 128
LANES = 128
SB_LOG_ZERO = -110.0
_SUB_SHIFT = HG_SUB.bit_length() - 1
_DH_SHIFT = SB_DH.bit_length() - 1

VMEM_LIMIT = 56 * 1024 * 1024


def _params(*sem):
    return pltpu.CompilerParams(dimension_semantics=sem, vmem_limit_bytes=VMEM_LIMIT)


def _layer_spec(layer, shape):
    zeros = (0,) * len(shape)
    return pl.BlockSpec((None,) + tuple(shape), lambda *_: (layer,) + zeros,
                        pipeline_mode=pl.Buffered(1))


def _split2(a):
    hi = a.astype(BF16)
    lo = (a - hi.astype(F32)).astype(BF16)
    return hi, lo


def _dot(a, b):
    return jnp.dot(a, b, preferred_element_type=F32)


def _dot_nt(a, b):
    return lax.dot_general(a, b, (((1,), (1,)), ((), ())), preferred_element_type=F32)


def _dot_tn(a, b):
    return lax.dot_general(a, b, (((0,), (0,)), ((), ())), preferred_element_type=F32)


def _layer_norm(y, g, b):
    mu = jnp.mean(y, axis=-1, keepdims=True)
    yc = y - mu
    var = jnp.mean(yc * yc, axis=-1, keepdims=True)
    return yc * lax.rsqrt(var + LN_EPS) * g + b


def _hgrn_local(q, k, v, b):
    c = HG_CHUNK
    nsub = c // HG_SUB
    b_last = b[c - 1:c, :]

    zeros = lambda n: [jnp.zeros((n, HG_DK), F32)] if n else []
    g3 = lambda a: a.reshape(nsub, HG_SUB, a.shape[-1])
    b3 = g3(b)
    e_own = jnp.broadcast_to(b3[:, HG_SUB - 1:HG_SUB, :], b3.shape).reshape(c, HG_DK)
    khat = k * jnp.exp(jnp.minimum(e_own - b, 0.0))
    qparts, kparts = [], []
    for j in range(nsub - 1):
        lo = (j + 1) * HG_SUB
        e_j = b[lo - 1:lo, :]
        qparts.append(jnp.concatenate(
            zeros(lo) + [q[lo:, :] * jnp.exp(jnp.minimum(b[lo:, :] - e_j, 0.0))], axis=0))
        kparts.append(jnp.concatenate(
            zeros(lo - HG_SUB) + [khat[lo - HG_SUB:lo, :]] + zeros(c - lo), axis=0))
    a_off = _dot_nt(jnp.concatenate(qparts, axis=1).astype(BF16),
                    jnp.concatenate(kparts, axis=1).astype(BF16))

    q3, k3 = g3(q), g3(k)
    delta = (lax.broadcasted_iota(jnp.int32, (c, c), 0) - lax.broadcasted_iota(jnp.int32, (c, c), 1))
    a_diag = jnp.zeros((c, c), F32)
    for d in range(HG_SUB):
        if d == 0:
            term = q3 * k3
        else:
            bs = pltpu.roll(b3, d, axis=1)
            ks = pltpu.roll(k3, d, axis=1)
            term = q3 * ks * jnp.exp(jnp.minimum(b3 - bs, 0.0))
        a_d = jnp.sum(term.reshape(c, HG_DK), axis=-1, keepdims=True)
        a_diag = jnp.where(delta == d, a_d, a_diag)
    row_sub = lax.broadcasted_iota(jnp.int32, (c, c), 0) >> _SUB_SHIFT
    col_sub = lax.broadcasted_iota(jnp.int32, (c, c), 1) >> _SUB_SHIFT
    a = a_off + jnp.where(row_sub == col_sub, a_diag, 0.0)

    vb = v.astype(BF16)
    o_intra = _dot(a.astype(BF16), vb)
    q_dec = (q * jnp.exp(b)).astype(BF16)
    k_dec = (k * jnp.exp(b_last - b)).astype(BF16)
    return o_intra, q_dec, _dot_tn(vb, k_dec), jnp.exp(b_last)


def _hgrn_rows(q, g, k, v, ga, gn, st_ref, o_ref, row0, after_head):
    c = HG_CHUNK
    nchunks = q.shape[0] // c
    rows = [slice(ci * c, (ci + 1) * c) for ci in range(nchunks)]
    row = lax.broadcasted_iota(jnp.int32, (c, c), 0)
    col = lax.broadcasted_iota(jnp.int32, (c, c), 1)
    tri = jnp.where(col <= row, 1.0, 0.0).astype(BF16)

    bs = []
    for r in rows:
        gh, gl = _split2(g[r, :])
        bs.append(_dot(tri, gh) + _dot(tri, gl))

    for h in range(HG_HEADS):
        cols = slice(h * HG_DK, (h + 1) * HG_DK)
        local = [_hgrn_local(q[r, cols], k[r, cols], v[r, cols], b[:, cols]) for r, b in zip(rows, bs)]
        st = st_ref[h]
        for r, (o_intra, q_dec, st_inc, st_decay) in zip(rows, local):
            o = o_intra + _dot_nt(q_dec, st.astype(BF16))
            st = st * st_decay + st_inc
            ms = jnp.mean(o * o, axis=-1, keepdims=True)
            out = o * lax.rsqrt(ms + RMS_EPS) * gn[:, cols] * ga[r, cols]
            o_ref[row0 + r.start:row0 + r.stop, cols] = out.astype(o_ref.dtype)
        st_ref[h] = st
        after_head[h]()


def _inproj_kernel(layer, x_ref, w_ref, lbl_ref, gn_ref,
                   oa_ref, qb_ref, kb_ref, vb_ref, sga_ref, sgb_ref, st_ref):
    @pl.when(pl.program_id(0) == 0)
    def _():
        st_ref[...] = jnp.zeros_like(st_ref)

    ts = x_ref.shape[0]
    xb = x_ref[...].astype(BF16)

    def proj(c0, width, rows=slice(None)):
        return _dot(xb[rows, :], w_ref[:, c0:c0 + width])

    logits = lbl_ref[...]
    m = jnp.max(logits, axis=0, keepdims=True)
    e = jnp.exp(logits - m)
    sm = e / jnp.sum(e, axis=0, keepdims=True)
    csum = sm[0:1]
    for j in range(1, layer + 1):
        csum = csum + sm[j:j + 1]
    lb = csum - sm[0:1]

    def attention_piece(ref, c0, scale):
        def emit():
            ref[...] = (proj(c0, SB_WIDTH) * scale).astype(BF16)
        return emit

    def gate_piece(ref, c0, j):
        def emit():
            ref[:, j:j + HG_WIDTH] = jax.nn.sigmoid(proj(c0 + j, HG_WIDTH)).astype(BF16)
        return emit

    pieces = [attention_piece(qb_ref, _OFF_QB, SB_DH ** -0.5), attention_piece(kb_ref, _OFF_KB, 1.0),
              attention_piece(vb_ref, _OFF_VB, 1.0)]
    pieces += [gate_piece(sga_ref, _OFF_GATE_A, j) for j in range(0, D_MODEL, HG_WIDTH)]
    pieces += [gate_piece(sgb_ref, _OFF_GATE_B, j) for j in range(0, D_MODEL, HG_WIDTH)]
    per_head = -(-len(pieces) // HG_HEADS)

    def rest_of_projection(h):
        def emit():
            for piece in pieces[h * per_head:(h + 1) * per_head]:
                piece()
        return emit

    def hgrn_inputs(rows):
        vals = {}

        def queries():
            vals["q"] = jax.nn.silu(proj(_OFF_QA, HG_WIDTH, rows))

        def forget_gate():
            z = proj(_OFF_FA, HG_WIDTH, rows)
            f_gate = lb + (1.0 - lb) * jax.nn.sigmoid(z)
            vals["g"] = jnp.log(jnp.maximum(f_gate, F_FLOOR))
            vals["k"] = (1.0 - lb) * jax.nn.sigmoid(-z)

        def inputs():
            vals["v"] = proj(_OFF_IA, HG_WIDTH, rows)

        def out_gate():
            vals["ga"] = jax.nn.silu(proj(_OFF_GA, HG_WIDTH, rows))

        return vals, [queries, forget_gate, inputs, out_gate]

    sub = ts // HG_SUBTILES if ts % (HG_SUBTILES * HG_CHUNK) == 0 else ts
    subtiles = [slice(r, r + sub) for r in range(0, ts, sub)]
    vals, steps = hgrn_inputs(subtiles[0])
    for step in steps:
        step()
    for n, rows in enumerate(subtiles):
        if n + 1 < len(subtiles):
            next_vals, hooks = hgrn_inputs(subtiles[n + 1])
        else:
            next_vals, hooks = None, [rest_of_projection(h) for h in range(HG_HEADS)]
        _hgrn_rows(vals["q"], vals["g"], vals["k"], vals["v"], vals["ga"], gn_ref[...], st_ref,
                   oa_ref, rows.start, hooks)
        vals = next_vals


def _inproj_hgrn(x, w_in, lb_logits, gn, layer, ts):
    s = x.shape[0]
    depth = lb_logits.shape[0]
    row = lambda w: pl.BlockSpec((ts, w), lambda i: (i, 0))
    bf16o = lambda w: jax.ShapeDtypeStruct((s, w), BF16)
    return pl.pallas_call(
        functools.partial(_inproj_kernel, layer),
        grid=(s // ts,),
        in_specs=[row(D_MODEL), _layer_spec(layer, (D_MODEL, IN_COLS)),
                  pl.BlockSpec((depth, HG_WIDTH), lambda i: (0, 0)),
                  _layer_spec(layer, (1, HG_WIDTH))],
        out_specs=[row(HG_WIDTH)] + [row(SB_WIDTH)] * 3 + [row(D_MODEL)] * 2,
        out_shape=[bf16o(HG_WIDTH)] + [bf16o(SB_WIDTH)] * 3 + [bf16o(D_MODEL)] * 2,
        scratch_shapes=[pltpu.VMEM((HG_HEADS, HG_DV, HG_DK), F32)],
        compiler_params=_params("arbitrary"),
        name="inproj_hgrn2",
    )(x, w_in, lb_logits, gn)


SB_QROWS = 64
SB_STEP_ROWS = 1024
SB_FIRST_TILES = 2
SB_LOOP_TILES = 2


def _sb_round(qms, lane_masks, k_ref, v_ref, tri2, tiles, carries):
    tq = qms[0].shape[0]
    log_fail, log_beta = [], []
    for qm, chain_tiles in zip(qms, tiles):
        for rows, mask, valid in chain_tiles:
            z = _dot_nt(qm, k_ref[rows, :])
            lg = jnp.log(1.0 + jnp.exp(-jnp.abs(z)))
            lf = jnp.minimum(-z, 0.0) - lg
            log_beta.append(lf + z)
            if mask is not None:
                lf = jnp.where(mask, lf, 0.0)
            if valid is not None:
                lf = lf * valid
            log_fail.append(lf)
    parts = [jnp.concatenate(_split2(lf), axis=1) for lf in log_fail]
    suffix = _dot(jnp.concatenate(parts, axis=0), tri2)

    out = []
    idx = 0
    for n, chain_tiles in enumerate(tiles):
        carry = carries[n]
        ws, vs = [], []
        for rows, mask, valid in chain_tiles:
            w = jnp.exp(log_beta[idx] + suffix[idx * tq:(idx + 1) * tq, :] + carry)
            if mask is not None:
                w = jnp.where(mask, w, 0.0)
            if valid is not None:
                w = w * valid
            ws.append(w.astype(BF16))
            v = v_ref[rows, :]
            vs.append(jnp.where(lane_masks[n], v, jnp.zeros_like(v)))
            carry = carry + jnp.sum(log_fail[idx], axis=-1, keepdims=True)
            idx += 1
        out.append((_dot(jnp.concatenate(ws, axis=1), jnp.concatenate(vs, axis=0)), carry))
    return out


def _sb_kernel(q_ref, k_ref, v_ref, o_ref):
    tq, tk = SB_QROWS, SB_BLOCK
    nheads = LANES // SB_DH
    nq = q_ref.shape[0] // tq
    first_row = pl.program_id(1) * q_ref.shape[0]
    lane = lax.broadcasted_iota(jnp.int32, (1, LANES), 1)
    col_minus_row = (lax.broadcasted_iota(jnp.int32, (tq, tk), 1)
                     - lax.broadcasted_iota(jnp.int32, (tq, tk), 0))
    col = lax.broadcasted_iota(jnp.int32, (1, tk), 1)
    r2 = lax.broadcasted_iota(jnp.int32, (2 * tk, tk), 0) & (tk - 1)
    c2 = lax.broadcasted_iota(jnp.int32, (2 * tk, tk), 1)
    tri2 = jnp.where(r2 > c2, 1.0, 0.0).astype(BF16)
    lane_masks = [(lane >> _DH_SHIFT) == h for h in range(nheads)]
    chains = [(a, h) for a in range(nq) for h in range(nheads)]
    qms = []
    for a, h in chains:
        q = q_ref[a * tq:(a + 1) * tq, :]
        qms.append(jnp.where(lane_masks[h], q, jnp.zeros_like(q)))

    def tiles_from(a, offset, count, diagonal_first):
        q0 = first_row + a * tq
        tiles = []
        for t in range(count):
            start = q0 - (tk - tq) - (offset + t) * tk
            lo = jnp.maximum(start, 0)
            rows = pl.ds(pl.multiple_of(lo, tq), tk)
            if diagonal_first and t == 0:
                tiles.append((rows, col_minus_row < q0 - lo, None))
            else:
                tiles.append((rows, None, (col < start + tk - lo).astype(F32)))
        return tiles

    def sweep(offset, count, diagonal_first, carries):
        return _sb_round(qms, [lane_masks[h] for _, h in chains], k_ref, v_ref, tri2,
                         [tiles_from(a, offset, count, diagonal_first) for a, _ in chains], carries)

    def cmax_of(carries):
        return jnp.max(functools.reduce(jnp.maximum, carries))

    res = sweep(0, SB_FIRST_TILES, True, [jnp.zeros((tq, 1), F32)] * len(chains))
    accs = tuple(pv for pv, _ in res)
    carries = tuple(c for _, c in res)

    def cond(state):
        offset, _, _, cmax = state
        last_end = first_row + (nq - 1) * tq + tq - offset * tk
        return (last_end > 0) & (cmax > SB_LOG_ZERO)

    def body(state):
        offset, accs, carries, _ = state
        res = sweep(offset, SB_LOOP_TILES, False, carries)
        accs = tuple(acc + pv for acc, (pv, _) in zip(accs, res))
        carries = tuple(c for _, c in res)
        return offset + SB_LOOP_TILES, accs, carries, cmax_of(carries)

    _, accs, _, _ = lax.while_loop(
        cond, body, (jnp.int32(SB_FIRST_TILES), accs, carries, cmax_of(carries)))
    for a in range(nq):
        o_ref[a * tq:(a + 1) * tq, :] = functools.reduce(
            jnp.add, [accs[n] for n, (qa, _) in enumerate(chains) if qa == a]).astype(o_ref.dtype)


def _sb_attention(qb, kb, vb):
    s = qb.shape[0]
    rows = next(r for r in (SB_STEP_ROWS, SB_BLOCK) if s % r == 0)
    kv = pl.BlockSpec((s, LANES), lambda p, i: (0, p))
    qo = pl.BlockSpec((rows, LANES), lambda p, i: (i, p))
    return pl.pallas_call(
        _sb_kernel,
        grid=(SB_WIDTH // LANES, s // rows),
        in_specs=[qo, kv, kv],
        out_specs=qo,
        out_shape=jax.ShapeDtypeStruct((s, SB_WIDTH), BF16),
        compiler_params=_params("arbitrary", "arbitrary"),
        name="stickbreaking",
    )(qb, kb, vb)


def _merge_kernel(alpha, x_ref, oa_ref, ob_ref, sga_ref, sgb_ref, wa_ref, wb_ref, wo_ref,
                  g_ref, b_ref, o_ref):
    ya = _dot(oa_ref[...], wa_ref[...])
    yb = _dot(ob_ref[...], wb_ref[...])
    merged = sga_ref[...].astype(F32) * ya + sgb_ref[...].astype(F32) * yb
    y = alpha * x_ref[...] + _dot(merged.astype(BF16), wo_ref[...])
    o_ref[...] = _layer_norm(y, g_ref[...], b_ref[...])


def _merge(x, oa, ob, sga, sgb, w_a, w_b, w_out, ln_g, ln_b, alpha, layer, ts):
    s = x.shape[0]
    row = lambda w: pl.BlockSpec((ts, w), lambda i: (i, 0))
    return pl.pallas_call(
        functools.partial(_merge_kernel, alpha),
        grid=(s // ts,),
        in_specs=[row(D_MODEL), row(HG_WIDTH), row(SB_WIDTH), row(D_MODEL), row(D_MODEL),
                  _layer_spec(layer, (HG_WIDTH, D_MODEL)), _layer_spec(layer, (SB_WIDTH, D_MODEL)),
                  _layer_spec(layer, (D_MODEL, D_MODEL)), _layer_spec(layer, (1, D_MODEL)),
                  _layer_spec(layer, (1, D_MODEL))],
        out_specs=row(D_MODEL),
        out_shape=jax.ShapeDtypeStruct((s, D_MODEL), F32),
        compiler_params=_params("arbitrary"),
        name="merge_ln1",
    )(x, oa, ob, sga, sgb, w_a, w_b, w_out, ln_g, ln_b)


FF_CHUNK = 256
CARRY_ROWS = 8


def _ffn_kernel(alpha, x_ref, p_ref, wup_ref, cw_ref, cb_ref, wdn_ref, wpe_ref, wpg_ref,
                g_ref, b_ref, o_ref, carry_ref):
    ts = x_ref.shape[0]

    @pl.when(pl.program_id(0) == 0)
    def _():
        carry_ref[...] = jnp.zeros_like(carry_ref)

    x1 = x_ref[...]
    xb = x1.astype(BF16)
    row = lax.broadcasted_iota(jnp.int32, (ts, FF_CHUNK), 0)

    def up(c0):
        return _dot(xb, wup_ref[:, c0:c0 + FF_CHUNK])

    def conv(u, c0):
        cols = slice(c0, c0 + FF_CHUNK)
        prev = carry_ref[:, cols]
        p1 = prev[CARRY_ROWS - 1:CARRY_ROWS, :]
        p2 = prev[CARRY_ROWS - 2:CARRY_ROWS - 1, :]
        u1 = jnp.where(row == 0, p1, pltpu.roll(u, 1, axis=0))
        u2 = jnp.where(row == 0, p2, jnp.where(row == 1, p1, pltpu.roll(u, 2, axis=0)))
        carry_ref[:, cols] = u[ts - CARRY_ROWS:, :]
        cw = cw_ref[:, cols]
        return cb_ref[:, cols] + cw[0:1, :] * u2 + cw[1:2, :] * u1 + cw[2:3, :] * u

    nchunks = D_FF // FF_CHUNK
    hs = []
    for c in range(nchunks):
        c_val = conv(up(c * FF_CHUNK), c * FF_CHUNK)
        c_gate = conv(up(D_FF + c * FF_CHUNK), D_FF + c * FF_CHUNK)
        hs.append((jax.nn.gelu(c_gate) * c_val).astype(BF16))
    ffn = _dot(jnp.concatenate(hs, axis=1), wdn_ref[...])

    ple = _dot(p_ref[...].astype(BF16), wpe_ref[...]) * jax.nn.sigmoid(_dot(xb, wpg_ref[...]))
    o_ref[...] = _layer_norm(alpha * x1 + ffn + ple, g_ref[...], b_ref[...])


def _ffn(x1, p, bi, w_up, conv_w, conv_b, w_down, w_pe, w_pg, ln_g, ln_b, alpha, layer, ts):
    s = x1.shape[0]
    row = lambda w: pl.BlockSpec((ts, w), lambda i: (i, 0))
    return pl.pallas_call(
        functools.partial(_ffn_kernel, alpha),
        grid=(s // ts,),
        in_specs=[row(D_MODEL), pl.BlockSpec((None, None, ts, PLE_DIM), lambda i: (layer, bi, i, 0)),
                  _layer_spec(layer, (D_MODEL, 2 * D_FF)), _layer_spec(layer, (CONV_W, 2 * D_FF)),
                  _layer_spec(layer, (1, 2 * D_FF)), _layer_spec(layer, (D_FF, D_MODEL)),
                  _layer_spec(layer, (PLE_DIM, D_MODEL)), _layer_spec(layer, (D_MODEL, D_MODEL)),
                  _layer_spec(layer, (1, D_MODEL)), _layer_spec(layer, (1, D_MODEL))],
        out_specs=row(D_MODEL),
        out_shape=jax.ShapeDtypeStruct((s, D_MODEL), F32),
        scratch_shapes=[pltpu.VMEM((CARRY_ROWS, 2 * D_FF), F32)],
        compiler_params=_params("arbitrary"),
        name="convffn_ln2",
    )(x1, p, w_up, conv_w, conv_b, w_down, w_pe, w_pg, ln_g, ln_b)


def _tiles(s):
    return next(t for t in (512, 256, SB_BLOCK) if s % t == 0)


def kernel(x, p, lb_logits, w_in, hg_norm_g, w_a, w_b, w_out, ln1_g, ln1_b,
           w_up, conv_w, conv_b, w_down, w_pe, w_pg, ln2_g, ln2_b):
    batch, s, _ = x.shape
    depth = w_in.shape[0]
    assert s % SB_BLOCK == 0 and s % HG_CHUNK == 0
    alpha = float((2 * depth) ** 0.25)
    ts = _tiles(s)
    lbl = lb_logits.astype(F32)
    vec = lambda a: a.reshape(depth, 1, a.shape[-1])
    w_in, w_a, w_b, w_out, w_up, w_down, w_pe, w_pg = (
        w.astype(BF16) for w in (w_in, w_a, w_b, w_out, w_up, w_down, w_pe, w_pg))
    outs = []
    for bi in range(batch):
        h = x[bi]
        for i in range(depth):
            oa, qb, kb, vb, sga, sgb = _inproj_hgrn(h, w_in, lbl, vec(hg_norm_g), i, ts)
            ob = _sb_attention(qb, kb, vb)
            x1 = _merge(h, oa, ob, sga, sgb, w_a, w_b, w_out, vec(ln1_g), vec(ln1_b), alpha, i, ts)
            h = _ffn(x1, p, bi, w_up, conv_w, vec(conv_b), w_down, w_pe, w_pg,
                     vec(ln2_g), vec(ln2_b), alpha, i, ts)
        outs.append(h)
    return jnp.stack(outs, axis=0)
```

```python
import functools

import jax
import jax.numpy as jnp
from jax import lax
from jax.experimental import pallas as pl
from jax.experimental.pallas import tpu as pltpu

F32 = jnp.float32
BF16 = jnp.bfloat16

D_MODEL = 1024
PLE_DIM = 256
HG_HEADS = 4
HG_DK = 128
HG_DV = 128
HG_WIDTH = HG_HEADS * HG_DV
SB_HEADS = 8
SB_DH = 64
SB_WIDTH = SB_HEADS * SB_DH
D_FF = 2816
CONV_W = 3
LN_EPS = 1e-5
RMS_EPS = 1e-6
F_FLOOR = 1e-30

_OFF_QA = 0
_OFF_FA = _OFF_QA + HG_HEADS * HG_DK
_OFF_IA = _OFF_FA + HG_HEADS * HG_DK
_OFF_GA = _OFF_IA + HG_WIDTH
_OFF_QB = _OFF_GA + HG_WIDTH
_OFF_KB = _OFF_QB + SB_WIDTH
_OFF_VB = _OFF_KB + SB_WIDTH
_OFF_GATE_A = _OFF_VB + SB_WIDTH
_OFF_GATE_B = _OFF_GATE_A + D_MODEL
IN_COLS = _OFF_GATE_B + D_MODEL

HG_CHUNK = 64
HG_SUB = 8
HG_SUBTILES = 1
SB_BLOCK = 128
LANES = 128
SB_LOG_ZERO = -110.0
_SUB_SHIFT = HG_SUB.bit_length() - 1
_DH_SHIFT = SB_DH.bit_length() - 1

VMEM_LIMIT = 56 * 1024 * 1024


def _params(*sem):
    return pltpu.CompilerParams(dimension_semantics=sem, vmem_limit_bytes=VMEM_LIMIT)


def _layer_spec(layer, shape):
    zeros = (0,) * len(shape)
    return pl.BlockSpec((None,) + tuple(shape), lambda *_: (layer,) + zeros,
                        pipeline_mode=pl.Buffered(1))


def _split2(a):
    hi = a.astype(BF16)
    lo = (a - hi.astype(F32)).astype(BF16)
    return hi, lo


def _dot(a, b):
    return jnp.dot(a, b, preferred_element_type=F32)


def _dot_nt(a, b):
    return lax.dot_general(a, b, (((1,), (1,)), ((), ())), preferred_element_type=F32)


def _dot_tn(a, b):
    return lax.dot_general(a, b, (((0,), (0,)), ((), ())), preferred_element_type=F32)


def _layer_norm(y, g, b):
    mu = jnp.mean(y, axis=-1, keepdims=True)
    yc = y - mu
    var = jnp.mean(yc * yc, axis=-1, keepdims=True)
    return yc * lax.rsqrt(var + LN_EPS) * g + b


def _hgrn_local(q, k, v, b):
    c = HG_CHUNK
    nsub = c // HG_SUB
    b_last = b[c - 1:c, :]

    zeros = lambda n: [jnp.zeros((n, HG_DK), F32)] if n else []
    g3 = lambda a: a.reshape(nsub, HG_SUB, a.shape[-1])
    b3 = g3(b)
    e_own = jnp.broadcast_to(b3[:, HG_SUB - 1:HG_SUB, :], b3.shape).reshape(c, HG_DK)
    khat = k * jnp.exp(jnp.minimum(e_own - b, 0.0))
    qparts, kparts = [], []
    for j in range(nsub - 1):
        lo = (j + 1) * HG_SUB
        e_j = b[lo - 1:lo, :]
        qparts.append(jnp.concatenate(
            zeros(lo) + [q[lo:, :] * jnp.exp(jnp.minimum(b[lo:, :] - e_j, 0.0))], axis=0))
        kparts.append(jnp.concatenate(
            zeros(lo - HG_SUB) + [khat[lo - HG_SUB:lo, :]] + zeros(c - lo), axis=0))
    a_off = _dot_nt(jnp.concatenate(qparts, axis=1).astype(BF16),
                    jnp.concatenate(kparts, axis=1).astype(BF16))

    q3, k3 = g3(q), g3(k)
    delta = (lax.broadcasted_iota(jnp.int32, (c, c), 0) - lax.broadcasted_iota(jnp.int32, (c, c), 1))
    a_diag = jnp.zeros((c, c), F32)
    for d in range(HG_SUB):
        if d == 0:
            term = q3 * k3
        else:
            bs = pltpu.roll(b3, d, axis=1)
            ks = pltpu.roll(k3, d, axis=1)
            term = q3 * ks * jnp.exp(jnp.minimum(b3 - bs, 0.0))
        a_d = jnp.sum(term.reshape(c, HG_DK), axis=-1, keepdims=True)
        a_diag = jnp.where(delta == d, a_d, a_diag)
    row_sub = lax.broadcasted_iota(jnp.int32, (c, c), 0) >> _SUB_SHIFT
    col_sub = lax.broadcasted_iota(jnp.int32, (c, c), 1) >> _SUB_SHIFT
    a = a_off + jnp.where(row_sub == col_sub, a_diag, 0.0)

    vb = v.astype(BF16)
    o_intra = _dot(a.astype(BF16), vb)
    q_dec = (q * jnp.exp(b)).astype(BF16)
    k_dec = (k * jnp.exp(b_last - b)).astype(BF16)
    return o_intra, q_dec, _dot_tn(vb, k_dec), jnp.exp(b_last)


def _hgrn_rows(q, g, k, v, ga, gn, st_ref, o_ref, row0, after_head):
    c = HG_CHUNK
    nchunks = q.shape[0] // c
    rows = [slice(ci * c, (ci + 1) * c) for ci in range(nchunks)]
    row = lax.broadcasted_iota(jnp.int32, (c, c), 0)
    col = lax.broadcasted_iota(jnp.int32, (c, c), 1)
    tri = jnp.where(col <= row, 1.0, 0.0).astype(BF16)

    bs = []
    for r in rows:
        gh, gl = _split2(g[r, :])
        bs.append(_dot(tri, gh) + _dot(tri, gl))

    for h in range(HG_HEADS):
        cols = slice(h * HG_DK, (h + 1) * HG_DK)
        local = [_hgrn_local(q[r, cols], k[r, cols], v[r, cols], b[:, cols]) for r, b in zip(rows, bs)]
        st = st_ref[h]
        for r, (o_intra, q_dec, st_inc, st_decay) in zip(rows, local):
            o = o_intra + _dot_nt(q_dec, st.astype(BF16))
            st = st * st_decay + st_inc
            ms = jnp.mean(o * o, axis=-1, keepdims=True)
            out = o * lax.rsqrt(ms + RMS_EPS) * gn[:, cols] * ga[r, cols]
            o_ref[row0 + r.start:row0 + r.stop, cols] = out.astype(o_ref.dtype)
        st_ref[h] = st
        after_head[h]()


def _inproj_kernel(layer, x_ref, w_ref, lbl_ref, gn_ref,
                   oa_ref, qb_ref, kb_ref, vb_ref, sga_ref, sgb_ref, st_ref):
    @pl.when(pl.program_id(0) == 0)
    def _():
        st_ref[...] = jnp.zeros_like(st_ref)

    ts = x_ref.shape[0]
    xb = x_ref[...].astype(BF16)

    def proj(c0, width, rows=slice(None)):
        return _dot(xb[rows, :], w_ref[:, c0:c0 + width])

    logits = lbl_ref[...]
    m = jnp.max(logits, axis=0, keepdims=True)
    e = jnp.exp(logits - m)
    sm = e / jnp.sum(e, axis=0, keepdims=True)
    csum = sm[0:1]
    for j in range(1, layer + 1):
        csum = csum + sm[j:j + 1]
    lb = csum - sm[0:1]

    def attention_piece(ref, c0, scale):
        def emit():
            ref[...] = (proj(c0, SB_WIDTH) * scale).astype(BF16)
        return emit

    def gate_piece(ref, c0, j):
        def emit():
            ref[:, j:j + HG_WIDTH] = jax.nn.sigmoid(proj(c0 + j, HG_WIDTH)).astype(BF16)
        return emit

    pieces = [attention_piece(qb_ref, _OFF_QB, SB_DH ** -0.5), attention_piece(kb_ref, _OFF_KB, 1.0),
              attention_piece(vb_ref, _OFF_VB, 1.0)]
    pieces += [gate_piece(sga_ref, _OFF_GATE_A, j) for j in range(0, D_MODEL, HG_WIDTH)]
    pieces += [gate_piece(sgb_ref, _OFF_GATE_B, j) for j in range(0, D_MODEL, HG_WIDTH)]
    per_head = -(-len(pieces) // HG_HEADS)

    def rest_of_projection(h):
        def emit():
            for piece in pieces[h * per_head:(h + 1) * per_head]:
                piece()
        return emit

    def hgrn_inputs(rows):
        vals = {}

        def queries():
            vals["q"] = jax.nn.silu(proj(_OFF_QA, HG_WIDTH, rows))

        def forget_gate():
            z = proj(_OFF_FA, HG_WIDTH, rows)
            f_gate = lb + (1.0 - lb) * jax.nn.sigmoid(z)
            vals["g"] = jnp.log(jnp.maximum(f_gate, F_FLOOR))
            vals["k"] = (1.0 - lb) * jax.nn.sigmoid(-z)

        def inputs():
            vals["v"] = proj(_OFF_IA, HG_WIDTH, rows)

        def out_gate():
            vals["ga"] = jax.nn.silu(proj(_OFF_GA, HG_WIDTH, rows))

        return vals, [queries, forget_gate, inputs, out_gate]

    sub = ts // HG_SUBTILES if ts % (HG_SUBTILES * HG_CHUNK) == 0 else ts
    subtiles = [slice(r, r + sub) for r in range(0, ts, sub)]
    vals, steps = hgrn_inputs(subtiles[0])
    for step in steps:
        step()
    for n, rows in enumerate(subtiles):
        if n + 1 < len(subtiles):
            next_vals, hooks = hgrn_inputs(subtiles[n + 1])
        else:
            next_vals, hooks = None, [rest_of_projection(h) for h in range(HG_HEADS)]
        _hgrn_rows(vals["q"], vals["g"], vals["k"], vals["v"], vals["ga"], gn_ref[...], st_ref,
                   oa_ref, rows.start, hooks)
        vals = next_vals


def _inproj_hgrn(x, w_in, lb_logits, gn, layer, ts):
    s = x.shape[0]
    depth = lb_logits.shape[0]
    row = lambda w: pl.BlockSpec((ts, w), lambda i: (i, 0))
    bf16o = lambda w: jax.ShapeDtypeStruct((s, w), BF16)
    return pl.pallas_call(
        functools.partial(_inproj_kernel, layer),
        grid=(s // ts,),
        in_specs=[row(D_MODEL), _layer_spec(layer, (D_MODEL, IN_COLS)),
                  pl.BlockSpec((depth, HG_WIDTH), lambda i: (0, 0)),
                  _layer_spec(layer, (1, HG_WIDTH))],
        out_specs=[row(HG_WIDTH)] + [row(SB_WIDTH)] * 3 + [row(D_MODEL)] * 2,
        out_shape=[bf16o(HG_WIDTH)] + [bf16o(SB_WIDTH)] * 3 + [bf16o(D_MODEL)] * 2,
        scratch_shapes=[pltpu.VMEM((HG_HEADS, HG_DV, HG_DK), F32)],
        compiler_params=_params("arbitrary"),
        name="inproj_hgrn2",
    )(x, w_in, lb_logits, gn)


SB_QROWS = 64
SB_STEP_ROWS = 1024
SB_FIRST_TILES = 2
SB_LOOP_TILES = 2
SB_GROUP = 16


def _sb_round(qms, lane_masks, k_ref, v_ref, tri2, tiles, carries):
    tq = qms[0].shape[0]
    log_fail, log_beta = [], []
    for qm, chain_tiles in zip(qms, tiles):
        for rows, mask, valid in chain_tiles:
            z = _dot_nt(qm, k_ref[rows, :])
            lg = jnp.log(1.0 + jnp.exp(-jnp.abs(z)))
            lf = jnp.minimum(-z, 0.0) - lg
            log_beta.append(lf + z)
            if mask is not None:
                lf = jnp.where(mask, lf, 0.0)
            if valid is not None:
                lf = lf * valid
            log_fail.append(lf)
    parts = [jnp.concatenate(_split2(lf), axis=1) for lf in log_fail]
    suffix = _dot(jnp.concatenate(parts, axis=0), tri2)

    out = []
    idx = 0
    for n, chain_tiles in enumerate(tiles):
        carry = carries[n]
        ws, vs = [], []
        for rows, mask, valid in chain_tiles:
            w = jnp.exp(log_beta[idx] + suffix[idx * tq:(idx + 1) * tq, :] + carry)
            if mask is not None:
                w = jnp.where(mask, w, 0.0)
            if valid is not None:
                w = w * valid
            ws.append(w.astype(BF16))
            v = v_ref[rows, :]
            vs.append(jnp.where(lane_masks[n], v, jnp.zeros_like(v)))
            carry = carry + jnp.sum(log_fail[idx], axis=-1, keepdims=True)
            idx += 1
        out.append((_dot(jnp.concatenate(ws, axis=1), jnp.concatenate(vs, axis=0)), carry))
    return out


def _sb_kernel(q_ref, k_ref, v_ref, o_ref):
    tq, tk = SB_QROWS, SB_BLOCK
    nheads = LANES // SB_DH
    nq = q_ref.shape[0] // tq
    first_row = pl.program_id(1) * q_ref.shape[0]
    lane = lax.broadcasted_iota(jnp.int32, (1, LANES), 1)
    col_minus_row = (lax.broadcasted_iota(jnp.int32, (tq, tk), 1)
                     - lax.broadcasted_iota(jnp.int32, (tq, tk), 0))
    col = lax.broadcasted_iota(jnp.int32, (1, tk), 1)
    r2 = lax.broadcasted_iota(jnp.int32, (2 * tk, tk), 0) & (tk - 1)
    c2 = lax.broadcasted_iota(jnp.int32, (2 * tk, tk), 1)
    tri2 = jnp.where(r2 > c2, 1.0, 0.0).astype(BF16)
    lane_masks = [(lane >> _DH_SHIFT) == h for h in range(nheads)]
    chains = [(a, h) for a in range(nq) for h in range(nheads)]
    qms = []
    for a, h in chains:
        q = q_ref[a * tq:(a + 1) * tq, :]
        qms.append(jnp.where(lane_masks[h], q, jnp.zeros_like(q)))

    def tiles_from(a, offset, count, diagonal_first):
        q0 = first_row + a * tq
        tiles = []
        for t in range(count):
            start = q0 - (tk - tq) - (offset + t) * tk
            lo = jnp.maximum(start, 0)
            rows = pl.ds(pl.multiple_of(lo, tq), tk)
            if diagonal_first and t == 0:
                tiles.append((rows, col_minus_row < q0 - lo, None))
            else:
                tiles.append((rows, None, (col < start + tk - lo).astype(F32)))
        return tiles

    def sweep(offset, count, diagonal_first, carries):
        res = []
        for g in range(0, len(chains), SB_GROUP):
            grp = range(g, min(g + SB_GROUP, len(chains)))
            res += _sb_round([qms[n] for n in grp], [lane_masks[chains[n][1]] for n in grp],
                             k_ref, v_ref, tri2,
                             [tiles_from(chains[n][0], offset, count, diagonal_first) for n in grp],
                             [carries[n] for n in grp])
        return res

    def cmax_of(carries):
        return jnp.max(functools.reduce(jnp.maximum, carries))

    res = sweep(0, SB_FIRST_TILES, True, [jnp.zeros((tq, 1), F32)] * len(chains))
    accs = tuple(pv for pv, _ in res)
    carries = tuple(c for _, c in res)

    def cond(state):
        offset, _, _, cmax = state
        last_end = first_row + (nq - 1) * tq + tq - offset * tk
        return (last_end > 0) & (cmax > SB_LOG_ZERO)

    def body(state):
        offset, accs, carries, _ = state
        res = sweep(offset, SB_LOOP_TILES, False, carries)
        accs = tuple(acc + pv for acc, (pv, _) in zip(accs, res))
        carries = tuple(c for _, c in res)
        return offset + SB_LOOP_TILES, accs, carries, cmax_of(carries)

    _, accs, _, _ = lax.while_loop(
        cond, body, (jnp.int32(SB_FIRST_TILES), accs, carries, cmax_of(carries)))
    for a in range(nq):
        o_ref[a * tq:(a + 1) * tq, :] = functools.reduce(
            jnp.add, [accs[n] for n, (qa, _) in enumerate(chains) if qa == a]).astype(o_ref.dtype)


def _sb_attention(qb, kb, vb):
    s = qb.shape[0]
    rows = next(r for r in (SB_STEP_ROWS, SB_BLOCK) if s % r == 0)
    kv = pl.BlockSpec((s, LANES), lambda p, i: (0, p))
    qo = pl.BlockSpec((rows, LANES), lambda p, i: (i, p))
    return pl.pallas_call(
        _sb_kernel,
        grid=(SB_WIDTH // LANES, s // rows),
        in_specs=[qo, kv, kv],
        out_specs=qo,
        out_shape=jax.ShapeDtypeStruct((s, SB_WIDTH), BF16),
        compiler_params=_params("arbitrary", "arbitrary"),
        name="stickbreaking",
    )(qb, kb, vb)


FF_CHUNK = 256
CARRY_ROWS = 8


def _merge_ffn_kernel(alpha, x_ref, oa_ref, ob_ref, sga_ref, sgb_ref, p_ref,
                      wa_ref, wb_ref, wo_ref, g1_ref, b1_ref,
                      wup_ref, cw_ref, cb_ref, wdn_ref, wpe_ref, wpg_ref, g2_ref, b2_ref,
                      o_ref, carry_ref):
    ts = x_ref.shape[0]

    @pl.when(pl.program_id(0) == 0)
    def _():
        carry_ref[...] = jnp.zeros_like(carry_ref)

    ya = _dot(oa_ref[...], wa_ref[...])
    yb = _dot(ob_ref[...], wb_ref[...])
    merged = sga_ref[...].astype(F32) * ya + sgb_ref[...].astype(F32) * yb
    x1 = _layer_norm(alpha * x_ref[...] + _dot(merged.astype(BF16), wo_ref[...]),
                     g1_ref[...], b1_ref[...])
    xb = x1.astype(BF16)
    row = lax.broadcasted_iota(jnp.int32, (ts, FF_CHUNK), 0)

    def up(c0):
        return _dot(xb, wup_ref[:, c0:c0 + FF_CHUNK])

    def conv(u, c0):
        cols = slice(c0, c0 + FF_CHUNK)
        prev = carry_ref[:, cols]
        p1 = prev[CARRY_ROWS - 1:CARRY_ROWS, :]
        p2 = prev[CARRY_ROWS - 2:CARRY_ROWS - 1, :]
        u1 = jnp.where(row == 0, p1, pltpu.roll(u, 1, axis=0))
        u2 = jnp.where(row == 0, p2, jnp.where(row == 1, p1, pltpu.roll(u, 2, axis=0)))
        carry_ref[:, cols] = u[ts - CARRY_ROWS:, :]
        cw = cw_ref[:, cols]
        return cb_ref[:, cols] + cw[0:1, :] * u2 + cw[1:2, :] * u1 + cw[2:3, :] * u

    nchunks = D_FF // FF_CHUNK
    hs = []
    for c in range(nchunks):
        c_val = conv(up(c * FF_CHUNK), c * FF_CHUNK)
        c_gate = conv(up(D_FF + c * FF_CHUNK), D_FF + c * FF_CHUNK)
        hs.append((jax.nn.gelu(c_gate) * c_val).astype(BF16))
    ffn = _dot(jnp.concatenate(hs, axis=1), wdn_ref[...])

    ple = _dot(p_ref[...].astype(BF16), wpe_ref[...]) * jax.nn.sigmoid(_dot(xb, wpg_ref[...]))
    o_ref[...] = _layer_norm(alpha * x1 + ffn + ple, g2_ref[...], b2_ref[...])


def _merge_ffn(x, oa, ob, sga, sgb, p, bi, w_a, w_b, w_out, ln1_g, ln1_b,
               w_up, conv_w, conv_b, w_down, w_pe, w_pg, ln2_g, ln2_b, alpha, layer, ts):
    s = x.shape[0]
    row = lambda w: pl.BlockSpec((ts, w), lambda i: (i, 0))
    return pl.pallas_call(
        functools.partial(_merge_ffn_kernel, alpha),
        grid=(s // ts,),
        in_specs=[row(D_MODEL), row(HG_WIDTH), row(SB_WIDTH), row(D_MODEL), row(D_MODEL),
                  pl.BlockSpec((None, None, ts, PLE_DIM), lambda i: (layer, bi, i, 0)),
                  _layer_spec(layer, (HG_WIDTH, D_MODEL)), _layer_spec(layer, (SB_WIDTH, D_MODEL)),
                  _layer_spec(layer, (D_MODEL, D_MODEL)), _layer_spec(layer, (1, D_MODEL)),
                  _layer_spec(layer, (1, D_MODEL)),
                  _layer_spec(layer, (D_MODEL, 2 * D_FF)), _layer_spec(layer, (CONV_W, 2 * D_FF)),
                  _layer_spec(layer, (1, 2 * D_FF)), _layer_spec(layer, (D_FF, D_MODEL)),
                  _layer_spec(layer, (PLE_DIM, D_MODEL)), _layer_spec(layer, (D_MODEL, D_MODEL)),
                  _layer_spec(layer, (1, D_MODEL)), _layer_spec(layer, (1, D_MODEL))],
        out_specs=row(D_MODEL),
        out_shape=jax.ShapeDtypeStruct((s, D_MODEL), F32),
        scratch_shapes=[pltpu.VMEM((CARRY_ROWS, 2 * D_FF), F32)],
        compiler_params=_params("arbitrary"),
        name="merge_convffn",
    )(x, oa, ob, sga, sgb, p, w_a, w_b, w_out, ln1_g, ln1_b,
      w_up, conv_w, conv_b, w_down, w_pe, w_pg, ln2_g, ln2_b)


def _tiles(s):
    return next(t for t in (512, 256, SB_BLOCK) if s % t == 0)


def kernel(x, p, lb_logits, w_in, hg_norm_g, w_a, w_b, w_out, ln1_g, ln1_b,
           w_up, conv_w, conv_b, w_down, w_pe, w_pg, ln2_g, ln2_b):
    batch, s, _ = x.shape
    depth = w_in.shape[0]
    assert s % SB_BLOCK == 0 and s % HG_CHUNK == 0
    alpha = float((2 * depth) ** 0.25)
    ts = _tiles(s)
    lbl = lb_logits.astype(F32)
    vec = lambda a: a.reshape(depth, 1, a.shape[-1])
    w_in, w_a, w_b, w_out, w_up, w_down, w_pe, w_pg = (
        w.astype(BF16) for w in (w_in, w_a, w_b, w_out, w_up, w_down, w_pe, w_pg))
    outs = []
    for bi in range(batch):
        h = x[bi]
        for i in range(depth):
            oa, qb, kb, vb, sga, sgb = _inproj_hgrn(h, w_in, lbl, vec(hg_norm_g), i, ts)
            ob = _sb_attention(qb, kb, vb)
            h = _merge_ffn(h, oa, ob, sga, sgb, p, bi, w_a, w_b, w_out, vec(ln1_g), vec(ln1_b),
                           w_up, conv_w, vec(conv_b), w_down, w_pe, w_pg, vec(ln2_g), vec(ln2_b),
                           alpha, i, ts)
        outs.append(h)
    return jnp.stack(outs, axis=0)
```

```python
import functools

import jax
import jax.numpy as jnp
from jax import lax
from jax.experimental import pallas as pl
from jax.experimental.pallas import tpu as pltpu

F32 = jnp.float32
BF16 = jnp.bfloat16

D_MODEL = 1024
PLE_DIM = 256
HG_HEADS = 4
HG_DK = 128
HG_DV = 128
HG_WIDTH = HG_HEADS * HG_DV
SB_HEADS = 8
SB_DH = 64
SB_WIDTH = SB_HEADS * SB_DH
D_FF = 2816
CONV_W = 3
LN_EPS = 1e-5
RMS_EPS = 1e-6
F_FLOOR = 1e-30

_OFF_QA = 0
_OFF_FA = _OFF_QA + HG_HEADS * HG_DK
_OFF_IA = _OFF_FA + HG_HEADS * HG_DK
_OFF_GA = _OFF_IA + HG_WIDTH
_OFF_QB = _OFF_GA + HG_WIDTH
_OFF_KB = _OFF_QB + SB_WIDTH
_OFF_VB = _OFF_KB + SB_WIDTH
_OFF_GATE_A = _OFF_VB + SB_WIDTH
_OFF_GATE_B = _OFF_GATE_A + D_MODEL
IN_COLS = _OFF_GATE_B + D_MODEL

HG_CHUNK = 64
HG_SUB = 8
HG_SUBTILES = 1
SB_BLOCK = 128
LANES = 128
SB_LOG_ZERO = -110.0
_SUB_SHIFT = HG_SUB.bit_length() - 1
_DH_SHIFT = SB_DH.bit_length() - 1

VMEM_LIMIT = 56 * 1024 * 1024


def _params(*sem):
    return pltpu.CompilerParams(dimension_semantics=sem, vmem_limit_bytes=VMEM_LIMIT)


def _layer_spec(layer, shape):
    zeros = (0,) * len(shape)
    return pl.BlockSpec((None,) + tuple(shape), lambda *_: (layer,) + zeros,
                        pipeline_mode=pl.Buffered(1))


def _split2(a):
    hi = a.astype(BF16)
    lo = (a - hi.astype(F32)).astype(BF16)
    return hi, lo


def _dot(a, b):
    return jnp.dot(a, b, preferred_element_type=F32)


def _dot_nt(a, b):
    return lax.dot_general(a, b, (((1,), (1,)), ((), ())), preferred_element_type=F32)


def _dot_tn(a, b):
    return lax.dot_general(a, b, (((0,), (0,)), ((), ())), preferred_element_type=F32)


def _layer_norm(y, g, b):
    mu = jnp.mean(y, axis=-1, keepdims=True)
    yc = y - mu
    var = jnp.mean(yc * yc, axis=-1, keepdims=True)
    return yc * lax.rsqrt(var + LN_EPS) * g + b


def _hgrn_local(q, k, v, b):
    c = HG_CHUNK
    nsub = c // HG_SUB
    b_last = b[c - 1:c, :]

    zeros = lambda n: [jnp.zeros((n, HG_DK), F32)] if n else []
    g3 = lambda a: a.reshape(nsub, HG_SUB, a.shape[-1])
    b3 = g3(b)
    e_own = jnp.broadcast_to(b3[:, HG_SUB - 1:HG_SUB, :], b3.shape).reshape(c, HG_DK)
    khat = k * jnp.exp(jnp.minimum(e_own - b, 0.0))
    qparts, kparts = [], []
    for j in range(nsub - 1):
        lo = (j + 1) * HG_SUB
        e_j = b[lo - 1:lo, :]
        qparts.append(jnp.concatenate(
            zeros(lo) + [q[lo:, :] * jnp.exp(jnp.minimum(b[lo:, :] - e_j, 0.0))], axis=0))
        kparts.append(jnp.concatenate(
            zeros(lo - HG_SUB) + [khat[lo - HG_SUB:lo, :]] + zeros(c - lo), axis=0))
    a_off = _dot_nt(jnp.concatenate(qparts, axis=1).astype(BF16),
                    jnp.concatenate(kparts, axis=1).astype(BF16))

    q3, k3 = g3(q), g3(k)
    delta = (lax.broadcasted_iota(jnp.int32, (c, c), 0) - lax.broadcasted_iota(jnp.int32, (c, c), 1))
    a_diag = jnp.zeros((c, c), F32)
    for d in range(HG_SUB):
        if d == 0:
            term = q3 * k3
        else:
            bs = pltpu.roll(b3, d, axis=1)
            ks = pltpu.roll(k3, d, axis=1)
            term = q3 * ks * jnp.exp(jnp.minimum(b3 - bs, 0.0))
        a_d = jnp.sum(term.reshape(c, HG_DK), axis=-1, keepdims=True)
        a_diag = jnp.where(delta == d, a_d, a_diag)
    row_sub = lax.broadcasted_iota(jnp.int32, (c, c), 0) >> _SUB_SHIFT
    col_sub = lax.broadcasted_iota(jnp.int32, (c, c), 1) >> _SUB_SHIFT
    a = a_off + jnp.where(row_sub == col_sub, a_diag, 0.0)

    vb = v.astype(BF16)
    o_intra = _dot(a.astype(BF16), vb)
    q_dec = (q * jnp.exp(b)).astype(BF16)
    k_dec = (k * jnp.exp(b_last - b)).astype(BF16)
    return o_intra, q_dec, _dot_tn(vb, k_dec), jnp.exp(b_last)


def _hgrn_rows(q, g, k, v, ga, gn, st_ref, o_ref, row0, after_head):
    c = HG_CHUNK
    nchunks = q.shape[0] // c
    rows = [slice(ci * c, (ci + 1) * c) for ci in range(nchunks)]
    row = lax.broadcasted_iota(jnp.int32, (c, c), 0)
    col = lax.broadcasted_iota(jnp.int32, (c, c), 1)
    tri = jnp.where(col <= row, 1.0, 0.0).astype(BF16)

    bs = []
    for r in rows:
        gh, gl = _split2(g[r, :])
        bs.append(_dot(tri, gh) + _dot(tri, gl))

    for h in range(HG_HEADS):
        cols = slice(h * HG_DK, (h + 1) * HG_DK)
        local = [_hgrn_local(q[r, cols], k[r, cols], v[r, cols], b[:, cols]) for r, b in zip(rows, bs)]
        st = st_ref[h]
        for r, (o_intra, q_dec, st_inc, st_decay) in zip(rows, local):
            o = o_intra + _dot_nt(q_dec, st.astype(BF16))
            st = st * st_decay + st_inc
            ms = jnp.mean(o * o, axis=-1, keepdims=True)
            out = o * lax.rsqrt(ms + RMS_EPS) * gn[:, cols] * ga[r, cols]
            o_ref[row0 + r.start:row0 + r.stop, cols] = out.astype(o_ref.dtype)
        st_ref[h] = st
        after_head[h]()


def _inproj_kernel(layer, x_ref, w_ref, lbl_ref, gn_ref,
                   oa_ref, qb_ref, kb_ref, vb_ref, sga_ref, sgb_ref, st_ref):
    @pl.when(pl.program_id(0) == 0)
    def _():
        st_ref[...] = jnp.zeros_like(st_ref)

    ts = x_ref.shape[0]
    xb = x_ref[...].astype(BF16)

    def proj(c0, width, rows=slice(None)):
        return _dot(xb[rows, :], w_ref[:, c0:c0 + width])

    logits = lbl_ref[...]
    m = jnp.max(logits, axis=0, keepdims=True)
    e = jnp.exp(logits - m)
    sm = e / jnp.sum(e, axis=0, keepdims=True)
    csum = sm[0:1]
    for j in range(1, layer + 1):
        csum = csum + sm[j:j + 1]
    lb = csum - sm[0:1]

    def attention_piece(ref, c0, scale):
        def emit():
            ref[...] = (proj(c0, SB_WIDTH) * scale).astype(BF16)
        return emit

    def gate_piece(ref, c0, j):
        def emit():
            ref[:, j:j + HG_WIDTH] = jax.nn.sigmoid(proj(c0 + j, HG_WIDTH)).astype(BF16)
        return emit

    pieces = [attention_piece(qb_ref, _OFF_QB, SB_DH ** -0.5), attention_piece(kb_ref, _OFF_KB, 1.0),
              attention_piece(vb_ref, _OFF_VB, 1.0)]
    pieces += [gate_piece(sga_ref, _OFF_GATE_A, j) for j in range(0, D_MODEL, HG_WIDTH)]
    pieces += [gate_piece(sgb_ref, _OFF_GATE_B, j) for j in range(0, D_MODEL, HG_WIDTH)]
    per_head = -(-len(pieces) // HG_HEADS)

    def rest_of_projection(h):
        def emit():
            for piece in pieces[h * per_head:(h + 1) * per_head]:
                piece()
        return emit

    def hgrn_inputs(rows):
        vals = {}

        def queries():
            vals["q"] = jax.nn.silu(proj(_OFF_QA, HG_WIDTH, rows))

        def forget_gate():
            z = proj(_OFF_FA, HG_WIDTH, rows)
            f_gate = lb + (1.0 - lb) * jax.nn.sigmoid(z)
            vals["g"] = jnp.log(jnp.maximum(f_gate, F_FLOOR))
            vals["k"] = (1.0 - lb) * jax.nn.sigmoid(-z)

        def inputs():
            vals["v"] = proj(_OFF_IA, HG_WIDTH, rows)

        def out_gate():
            vals["ga"] = jax.nn.silu(proj(_OFF_GA, HG_WIDTH, rows))

        return vals, [queries, forget_gate, inputs, out_gate]

    sub = ts // HG_SUBTILES if ts % (HG_SUBTILES * HG_CHUNK) == 0 else ts
    subtiles = [slice(r, r + sub) for r in range(0, ts, sub)]
    vals, steps = hgrn_inputs(subtiles[0])
    for step in steps:
        step()
    for n, rows in enumerate(subtiles):
        if n + 1 < len(subtiles):
            next_vals, hooks = hgrn_inputs(subtiles[n + 1])
        else:
            next_vals, hooks = None, [rest_of_projection(h) for h in range(HG_HEADS)]
        _hgrn_rows(vals["q"], vals["g"], vals["k"], vals["v"], vals["ga"], gn_ref[...], st_ref,
                   oa_ref, rows.start, hooks)
        vals = next_vals


def _inproj_hgrn(x, w_in, lb_logits, gn, layer, ts):
    s = x.shape[0]
    depth = lb_logits.shape[0]
    row = lambda w: pl.BlockSpec((ts, w), lambda i: (i, 0))
    bf16o = lambda w: jax.ShapeDtypeStruct((s, w), BF16)
    return pl.pallas_call(
        functools.partial(_inproj_kernel, layer),
        grid=(s // ts,),
        in_specs=[row(D_MODEL), _layer_spec(layer, (D_MODEL, IN_COLS)),
                  pl.BlockSpec((depth, HG_WIDTH), lambda i: (0, 0)),
                  _layer_spec(layer, (1, HG_WIDTH))],
        out_specs=[row(HG_WIDTH)] + [row(SB_WIDTH)] * 3 + [row(D_MODEL)] * 2,
        out_shape=[bf16o(HG_WIDTH)] + [bf16o(SB_WIDTH)] * 3 + [bf16o(D_MODEL)] * 2,
        scratch_shapes=[pltpu.VMEM((HG_HEADS, HG_DV, HG_DK), F32)],
        compiler_params=_params("arbitrary"),
        name="inproj_hgrn2",
    )(x, w_in, lb_logits, gn)


SB_QROWS = 64
SB_STEP_ROWS = 1024
SB_FIRST_TILES = 2
SB_LOOP_TILES = 2
SB_GROUP = 16


def _sb_round(qms, lane_masks, tri2, tiles, carries):
    tq = qms[0].shape[0]
    log_fail, log_beta = [], []
    for qm, chain_tiles in zip(qms, tiles):
        for keys, _, mask, valid in chain_tiles:
            z = _dot_nt(qm, keys)
            lg = jnp.log(1.0 + jnp.exp(-jnp.abs(z)))
            lf = jnp.minimum(-z, 0.0) - lg
            log_beta.append(lf + z)
            if mask is not None:
                lf = jnp.where(mask, lf, 0.0)
            if valid is not None:
                lf = lf * valid
            log_fail.append(lf)
    parts = [jnp.concatenate(_split2(lf), axis=1) for lf in log_fail]
    suffix = _dot(jnp.concatenate(parts, axis=0), tri2)

    out = []
    idx = 0
    for n, chain_tiles in enumerate(tiles):
        carry = carries[n]
        ws, vs = [], []
        for _, v, mask, valid in chain_tiles:
            w = jnp.exp(log_beta[idx] + suffix[idx * tq:(idx + 1) * tq, :] + carry)
            if mask is not None:
                w = jnp.where(mask, w, 0.0)
            if valid is not None:
                w = w * valid
            ws.append(w.astype(BF16))
            vs.append(jnp.where(lane_masks[n], v, jnp.zeros_like(v)))
            carry = carry + jnp.sum(log_fail[idx], axis=-1, keepdims=True)
            idx += 1
        out.append((_dot(jnp.concatenate(ws, axis=1), jnp.concatenate(vs, axis=0)), carry))
    return out


def _sb_kernel(q_ref, k_ref, v_ref, o_ref):
    tq, tk = SB_QROWS, SB_BLOCK
    nheads = LANES // SB_DH
    nq = q_ref.shape[0] // tq
    first_row = pl.program_id(1) * q_ref.shape[0]
    lane = lax.broadcasted_iota(jnp.int32, (1, LANES), 1)
    col_minus_row = (lax.broadcasted_iota(jnp.int32, (tq, tk), 1)
                     - lax.broadcasted_iota(jnp.int32, (tq, tk), 0))
    col = lax.broadcasted_iota(jnp.int32, (1, tk), 1)
    r2 = lax.broadcasted_iota(jnp.int32, (2 * tk, tk), 0) & (tk - 1)
    c2 = lax.broadcasted_iota(jnp.int32, (2 * tk, tk), 1)
    tri2 = jnp.where(r2 > c2, 1.0, 0.0).astype(BF16)
    lane_masks = [(lane >> _DH_SHIFT) == h for h in range(nheads)]
    chains = [(a, h) for a in range(nq) for h in range(nheads)]
    qms = []
    for a, h in chains:
        q = q_ref[a * tq:(a + 1) * tq, :]
        qms.append(jnp.where(lane_masks[h], q, jnp.zeros_like(q)))

    def tiles_from(a, offset, count, diagonal_first):
        q0 = first_row + a * tq
        tiles = []
        for t in range(count):
            start = q0 - (tk - tq) - (offset + t) * tk
            lo = jnp.maximum(start, 0)
            rows = pl.ds(pl.multiple_of(lo, tq), tk)
            if diagonal_first and t == 0:
                mask, factor = col_minus_row < q0 - lo, None
            else:
                mask, factor = None, (col < start + tk - lo).astype(F32)
            tiles.append((k_ref[rows, :], v_ref[rows, :], mask, factor))
        return tiles

    def sweep(offset, count, diagonal_first, carries):
        res = []
        for g in range(0, len(chains), SB_GROUP):
            grp = range(g, min(g + SB_GROUP, len(chains)))
            res += _sb_round([qms[n] for n in grp], [lane_masks[chains[n][1]] for n in grp], tri2,
                             [tiles_from(chains[n][0], offset, count, diagonal_first) for n in grp],
                             [carries[n] for n in grp])
        return res

    def cmax_of(carries):
        return jnp.max(functools.reduce(jnp.maximum, carries))

    res = sweep(0, SB_FIRST_TILES, True, [jnp.zeros((tq, 1), F32)] * len(chains))
    accs = tuple(pv for pv, _ in res)
    carries = tuple(c for _, c in res)

    def cond(state):
        offset, _, _, cmax = state
        last_end = first_row + (nq - 1) * tq + tq - offset * tk
        return (last_end > 0) & (cmax > SB_LOG_ZERO)

    def body(state):
        offset, accs, carries, _ = state
        res = sweep(offset, SB_LOOP_TILES, False, carries)
        accs = tuple(acc + pv for acc, (pv, _) in zip(accs, res))
        carries = tuple(c for _, c in res)
        return offset + SB_LOOP_TILES, accs, carries, cmax_of(carries)

    _, accs, _, _ = lax.while_loop(
        cond, body, (jnp.int32(SB_FIRST_TILES), accs, carries, cmax_of(carries)))
    for a in range(nq):
        o_ref[a * tq:(a + 1) * tq, :] = functools.reduce(
            jnp.add, [accs[n] for n, (qa, _) in enumerate(chains) if qa == a]).astype(o_ref.dtype)


def _sb_attention(qb, kb, vb):
    s = qb.shape[0]
    rows = next(r for r in (SB_STEP_ROWS, SB_BLOCK) if s % r == 0)
    kv = pl.BlockSpec((s, LANES), lambda p, i: (0, p))
    qo = pl.BlockSpec((rows, LANES), lambda p, i: (i, p))
    return pl.pallas_call(
        _sb_kernel,
        grid=(SB_WIDTH // LANES, s // rows),
        in_specs=[qo, kv, kv],
        out_specs=qo,
        out_shape=jax.ShapeDtypeStruct((s, SB_WIDTH), BF16),
        compiler_params=_params("arbitrary", "arbitrary"),
        name="stickbreaking",
    )(qb, kb, vb)


SB_WINDOW_PREV = 256
SB_NOTHING_LEFT = -1e30


def _sb_window(q_ref, kprev_ref, kcur_ref, vprev_ref, vcur_ref, tile_row0):
    tq, tk = SB_QROWS, SB_BLOCK
    nheads = LANES // SB_DH
    nq = q_ref.shape[0] // tq
    lane = lax.broadcasted_iota(jnp.int32, (1, LANES), 1)
    col_minus_row = (lax.broadcasted_iota(jnp.int32, (tq, tk), 1)
                     - lax.broadcasted_iota(jnp.int32, (tq, tk), 0))
    col = lax.broadcasted_iota(jnp.int32, (1, tk), 1)
    r2 = lax.broadcasted_iota(jnp.int32, (2 * tk, tk), 0) & (tk - 1)
    c2 = lax.broadcasted_iota(jnp.int32, (2 * tk, tk), 1)
    tri2 = jnp.where(r2 > c2, 1.0, 0.0).astype(BF16)
    lane_masks = [(lane >> _DH_SHIFT) == h for h in range(nheads)]
    chains = [(a, h) for a in range(nq) for h in range(nheads)]
    assert SB_WINDOW_PREV >= (tk - tq) + (SB_FIRST_TILES - 1) * tk

    outs, cmax = [], None
    for p in range(SB_WIDTH // LANES):
        cols = slice(p * LANES, (p + 1) * LANES)
        kwin = jnp.concatenate([kprev_ref[:, cols], kcur_ref[:, cols]], axis=0)
        vwin = jnp.concatenate([vprev_ref[:, cols], vcur_ref[:, cols]], axis=0)
        qms, tiles = [], []
        for a, h in chains:
            q = q_ref[a * tq:(a + 1) * tq, cols]
            qms.append(jnp.where(lane_masks[h], q, jnp.zeros_like(q)))
            chain_tiles = []
            for t in range(SB_FIRST_TILES):
                w = SB_WINDOW_PREV + a * tq - (tk - tq) - t * tk
                live = col >= SB_WINDOW_PREV - w - tile_row0
                if t == 0:
                    mask, factor = (col_minus_row < tk - tq) & live, None
                else:
                    mask, factor = None, live.astype(F32)
                chain_tiles.append((kwin[w:w + tk, :], vwin[w:w + tk, :], mask, factor))
            tiles.append(chain_tiles)
        res = _sb_round(qms, [lane_masks[h] for _, h in chains], tri2, tiles,
                        [jnp.zeros((tq, 1), F32)] * len(chains))
        outs.append(jnp.concatenate(
            [functools.reduce(jnp.add, [res[n][0] for n, (qa, _) in enumerate(chains) if qa == a])
             for a in range(nq)], axis=0).astype(BF16))
        left = []
        for n, (a, _) in enumerate(chains):
            first_key = tile_row0 + a * tq - (tk - tq) - (SB_FIRST_TILES - 1) * tk
            left.append(res[n][1] + jnp.where(first_key > 0, 0.0, SB_NOTHING_LEFT))
        pair_max = jnp.max(functools.reduce(jnp.maximum, left))
        cmax = pair_max if cmax is None else jnp.maximum(cmax, pair_max)
    return jnp.concatenate(outs, axis=1), cmax


def _sb_fast_kernel(q_ref, kprev_ref, kcur_ref, vprev_ref, vcur_ref, o_ref, left_ref):
    ob, cmax = _sb_window(q_ref, kprev_ref, kcur_ref, vprev_ref, vcur_ref,
                          pl.program_id(0) * q_ref.shape[0])
    o_ref[...] = ob
    left_ref[...] = jnp.full(left_ref.shape, cmax, F32)


def _sb_attention_fast(qb, kb, vb, rows):
    s = qb.shape[0]
    tile = pl.BlockSpec((rows, SB_WIDTH), lambda i: (i, 0))
    per_tile = rows // SB_WINDOW_PREV
    prev = pl.BlockSpec((SB_WINDOW_PREV, SB_WIDTH), lambda i: (jnp.maximum(i * per_tile - 1, 0), 0))
    return pl.pallas_call(
        _sb_fast_kernel,
        grid=(s // rows,),
        in_specs=[tile, prev, tile, prev, tile],
        out_specs=[tile, pl.BlockSpec((None, 8, LANES), lambda i: (i, 0, 0))],
        out_shape=[jax.ShapeDtypeStruct((s, SB_WIDTH), BF16),
                   jax.ShapeDtypeStruct((s // rows, 8, LANES), F32)],
        compiler_params=_params("arbitrary"),
        name="stickbreaking_window",
    )(qb, kb, kb, vb, vb)


FF_CHUNK = 256
CARRY_ROWS = 8


def _merge_ffn_kernel(alpha, x_ref, oa_ref, ob_ref, sga_ref, sgb_ref, p_ref,
                      wa_ref, wb_ref, wo_ref, g1_ref, b1_ref,
                      wup_ref, cw_ref, cb_ref, wdn_ref, wpe_ref, wpg_ref, g2_ref, b2_ref,
                      o_ref, carry_ref):
    ts = x_ref.shape[0]

    @pl.when(pl.program_id(0) == 0)
    def _():
        carry_ref[...] = jnp.zeros_like(carry_ref)

    ya = _dot(oa_ref[...], wa_ref[...])
    yb = _dot(ob_ref[...], wb_ref[...])
    merged = sga_ref[...].astype(F32) * ya + sgb_ref[...].astype(F32) * yb
    x1 = _layer_norm(alpha * x_ref[...] + _dot(merged.astype(BF16), wo_ref[...]),
                     g1_ref[...], b1_ref[...])
    xb = x1.astype(BF16)
    row = lax.broadcasted_iota(jnp.int32, (ts, FF_CHUNK), 0)

    def up(c0):
        return _dot(xb, wup_ref[:, c0:c0 + FF_CHUNK])

    def conv(u, c0):
        cols = slice(c0, c0 + FF_CHUNK)
        prev = carry_ref[:, cols]
        p1 = prev[CARRY_ROWS - 1:CARRY_ROWS, :]
        p2 = prev[CARRY_ROWS - 2:CARRY_ROWS - 1, :]
        u1 = jnp.where(row == 0, p1, pltpu.roll(u, 1, axis=0))
        u2 = jnp.where(row == 0, p2, jnp.where(row == 1, p1, pltpu.roll(u, 2, axis=0)))
        carry_ref[:, cols] = u[ts - CARRY_ROWS:, :]
        cw = cw_ref[:, cols]
        return cb_ref[:, cols] + cw[0:1, :] * u2 + cw[1:2, :] * u1 + cw[2:3, :] * u

    nchunks = D_FF // FF_CHUNK
    hs = []
    for c in range(nchunks):
        c_val = conv(up(c * FF_CHUNK), c * FF_CHUNK)
        c_gate = conv(up(D_FF + c * FF_CHUNK), D_FF + c * FF_CHUNK)
        hs.append((jax.nn.gelu(c_gate) * c_val).astype(BF16))
    ffn = _dot(jnp.concatenate(hs, axis=1), wdn_ref[...])

    ple = _dot(p_ref[...].astype(BF16), wpe_ref[...]) * jax.nn.sigmoid(_dot(xb, wpg_ref[...]))
    o_ref[...] = _layer_norm(alpha * x1 + ffn + ple, g2_ref[...], b2_ref[...])


def _merge_ffn(x, oa, ob, sga, sgb, p, bi, w_a, w_b, w_out, ln1_g, ln1_b,
               w_up, conv_w, conv_b, w_down, w_pe, w_pg, ln2_g, ln2_b, alpha, layer, ts):
    s = x.shape[0]
    row = lambda w: pl.BlockSpec((ts, w), lambda i: (i, 0))
    return pl.pallas_call(
        functools.partial(_merge_ffn_kernel, alpha),
        grid=(s // ts,),
        in_specs=[row(D_MODEL), row(HG_WIDTH), row(SB_WIDTH), row(D_MODEL), row(D_MODEL),
                  pl.BlockSpec((None, None, ts, PLE_DIM), lambda i: (layer, bi, i, 0)),
                  _layer_spec(layer, (HG_WIDTH, D_MODEL)), _layer_spec(layer, (SB_WIDTH, D_MODEL)),
                  _layer_spec(layer, (D_MODEL, D_MODEL)), _layer_spec(layer, (1, D_MODEL)),
                  _layer_spec(layer, (1, D_MODEL)),
                  _layer_spec(layer, (D_MODEL, 2 * D_FF)), _layer_spec(layer, (CONV_W, 2 * D_FF)),
                  _layer_spec(layer, (1, 2 * D_FF)), _layer_spec(layer, (D_FF, D_MODEL)),
                  _layer_spec(layer, (PLE_DIM, D_MODEL)), _layer_spec(layer, (D_MODEL, D_MODEL)),
                  _layer_spec(layer, (1, D_MODEL)), _layer_spec(layer, (1, D_MODEL))],
        out_specs=row(D_MODEL),
        out_shape=jax.ShapeDtypeStruct((s, D_MODEL), F32),
        scratch_shapes=[pltpu.VMEM((CARRY_ROWS, 2 * D_FF), F32)],
        compiler_params=_params("arbitrary"),
        name="merge_convffn",
    )(x, oa, ob, sga, sgb, p, w_a, w_b, w_out, ln1_g, ln1_b,
      w_up, conv_w, conv_b, w_down, w_pe, w_pg, ln2_g, ln2_b)


def _tiles(s):
    return next(t for t in (512, 256, SB_BLOCK) if s % t == 0)


def kernel(x, p, lb_logits, w_in, hg_norm_g, w_a, w_b, w_out, ln1_g, ln1_b,
           w_up, conv_w, conv_b, w_down, w_pe, w_pg, ln2_g, ln2_b):
    batch, s, _ = x.shape
    depth = w_in.shape[0]
    assert s % SB_BLOCK == 0 and s % HG_CHUNK == 0
    alpha = float((2 * depth) ** 0.25)
    ts = _tiles(s)
    lbl = lb_logits.astype(F32)
    vec = lambda a: a.reshape(depth, 1, a.shape[-1])
    w_in, w_a, w_b, w_out, w_up, w_down, w_pe, w_pg = (
        w.astype(BF16) for w in (w_in, w_a, w_b, w_out, w_up, w_down, w_pe, w_pg))
    outs = []
    for bi in range(batch):
        h = x[bi]
        for i in range(depth):
            oa, qb, kb, vb, sga, sgb = _inproj_hgrn(h, w_in, lbl, vec(hg_norm_g), i, ts)
            if ts % SB_WINDOW_PREV == 0:
                ob, left = _sb_attention_fast(qb, kb, vb, ts)
                ob = lax.cond(jnp.max(left) > SB_LOG_ZERO,
                              lambda qb=qb, kb=kb, vb=vb: _sb_attention(qb, kb, vb), lambda ob=ob: ob)
            else:
                ob = _sb_attention(qb, kb, vb)
            h = _merge_ffn(h, oa, ob, sga, sgb, p, bi, w_a, w_b, w_out, vec(ln1_g), vec(ln1_b),
                           w_up, conv_w, vec(conv_b), w_down, w_pe, w_pg, vec(ln2_g), vec(ln2_b),
                           alpha, i, ts)
        outs.append(h)
    return jnp.stack(outs, axis=0)
```

```python
import functools

import jax
import jax.numpy as jnp
from jax import lax
from jax.experimental import pallas as pl
from jax.experimental.pallas import tpu as pltpu

F32 = jnp.float32
BF16 = jnp.bfloat16

D_MODEL = 1024
PLE_DIM = 256
HG_HEADS = 4
HG_DK = 128
HG_DV = 128
HG_WIDTH = HG_HEADS * HG_DV
SB_HEADS = 8
SB_DH = 64
SB_WIDTH = SB_HEADS * SB_DH
D_FF = 2816
CONV_W = 3
LN_EPS = 1e-5
RMS_EPS = 1e-6
F_FLOOR = 1e-30

_OFF_QA = 0
_OFF_FA = _OFF_QA + HG_HEADS * HG_DK
_OFF_IA = _OFF_FA + HG_HEADS * HG_DK
_OFF_GA = _OFF_IA + HG_WIDTH
_OFF_QB = _OFF_GA + HG_WIDTH
_OFF_KB = _OFF_QB + SB_WIDTH
_OFF_VB = _OFF_KB + SB_WIDTH
_OFF_GATE_A = _OFF_VB + SB_WIDTH
_OFF_GATE_B = _OFF_GATE_A + D_MODEL
IN_COLS = _OFF_GATE_B + D_MODEL

HG_CHUNK = 64
HG_SUB = 8
SB_BLOCK = 128
LANES = 128
SB_LOG_ZERO = -110.0
_SUB_SHIFT = HG_SUB.bit_length() - 1
_DH_SHIFT = SB_DH.bit_length() - 1

VMEM_LIMIT = 56 * 1024 * 1024


def _params(*sem):
    return pltpu.CompilerParams(dimension_semantics=sem, vmem_limit_bytes=VMEM_LIMIT)


def _layer_spec(layer, shape):
    zeros = (0,) * len(shape)
    return pl.BlockSpec((None,) + tuple(shape), lambda *_: (layer,) + zeros,
                        pipeline_mode=pl.Buffered(1))


def _split2(a):
    hi = a.astype(BF16)
    lo = (a - hi.astype(F32)).astype(BF16)
    return hi, lo


def _dot(a, b):
    return jnp.dot(a, b, preferred_element_type=F32)


def _dot_nt(a, b):
    return lax.dot_general(a, b, (((1,), (1,)), ((), ())), preferred_element_type=F32)


def _dot_tn(a, b):
    return lax.dot_general(a, b, (((0,), (0,)), ((), ())), preferred_element_type=F32)


def _layer_norm(y, g, b):
    mu = jnp.mean(y, axis=-1, keepdims=True)
    yc = y - mu
    var = jnp.mean(yc * yc, axis=-1, keepdims=True)
    return yc * lax.rsqrt(var + LN_EPS) * g + b


def _hgrn_local(q, k, v, b):
    c = HG_CHUNK
    nsub = c // HG_SUB
    b_last = b[c - 1:c, :]

    zeros = lambda n: [jnp.zeros((n, HG_DK), F32)] if n else []
    g3 = lambda a: a.reshape(nsub, HG_SUB, a.shape[-1])
    b3 = g3(b)
    e_own = jnp.broadcast_to(b3[:, HG_SUB - 1:HG_SUB, :], b3.shape).reshape(c, HG_DK)
    khat = k * jnp.exp(e_own - b)
    qparts, kparts = [], []
    for j in range(nsub - 1):
        lo = (j + 1) * HG_SUB
        e_j = b[lo - 1:lo, :]
        qparts.append(jnp.concatenate(
            zeros(lo) + [q[lo:, :] * jnp.exp(b[lo:, :] - e_j)], axis=0))
        kparts.append(jnp.concatenate(
            zeros(lo - HG_SUB) + [khat[lo - HG_SUB:lo, :]] + zeros(c - lo), axis=0))
    a_off = _dot_nt(jnp.concatenate(qparts, axis=1).astype(BF16),
                    jnp.concatenate(kparts, axis=1).astype(BF16))

    q3, k3 = g3(q), g3(k)
    delta = (lax.broadcasted_iota(jnp.int32, (c, c), 0) - lax.broadcasted_iota(jnp.int32, (c, c), 1))
    a_diag = jnp.zeros((c, c), F32)
    for d in range(HG_SUB):
        if d == 0:
            term = q3 * k3
        else:
            bs = pltpu.roll(b3, d, axis=1)
            ks = pltpu.roll(k3, d, axis=1)
            term = q3 * ks * jnp.exp(jnp.minimum(b3 - bs, 0.0))
        a_d = jnp.sum(term.reshape(c, HG_DK), axis=-1, keepdims=True)
        a_diag = jnp.where(delta == d, a_d, a_diag)
    row_sub = lax.broadcasted_iota(jnp.int32, (c, c), 0) >> _SUB_SHIFT
    col_sub = lax.broadcasted_iota(jnp.int32, (c, c), 1) >> _SUB_SHIFT
    a = a_off + jnp.where(row_sub == col_sub, a_diag, 0.0)

    vb = v.astype(BF16)
    o_intra = _dot(a.astype(BF16), vb)
    q_dec = (q * jnp.exp(b)).astype(BF16)
    k_dec = (k * jnp.exp(b_last - b)).astype(BF16)
    return o_intra, q_dec, _dot_tn(vb, k_dec), jnp.exp(b_last)


def _hgrn_rows(q, g, k, v, ga, gn, st_ref, o_ref, after_head):
    c = HG_CHUNK
    nchunks = q.shape[0] // c
    rows = [slice(ci * c, (ci + 1) * c) for ci in range(nchunks)]
    row = lax.broadcasted_iota(jnp.int32, (c, c), 0)
    col = lax.broadcasted_iota(jnp.int32, (c, c), 1)
    tri = jnp.where(col <= row, 1.0, 0.0).astype(BF16)

    bs = []
    for r in rows:
        gh, gl = _split2(g[r, :])
        bs.append(_dot(tri, gh) + _dot(tri, gl))

    for h in range(HG_HEADS):
        cols = slice(h * HG_DK, (h + 1) * HG_DK)
        local = [_hgrn_local(q[r, cols], k[r, cols], v[r, cols], b[:, cols]) for r, b in zip(rows, bs)]
        st = st_ref[h]
        for r, (o_intra, q_dec, st_inc, st_decay) in zip(rows, local):
            o = o_intra + _dot_nt(q_dec, st.astype(BF16))
            st = st * st_decay + st_inc
            ms = jnp.mean(o * o, axis=-1, keepdims=True)
            out = o * lax.rsqrt(ms + RMS_EPS) * gn[:, cols] * ga[r, cols]
            o_ref[r, cols] = out.astype(o_ref.dtype)
        st_ref[h] = st
        after_head[h]()


def _inproj_kernel(layer, x_ref, w_ref, lbl_ref, gn_ref,
                   oa_ref, qb_ref, kb_ref, vb_ref, sga_ref, sgb_ref, st_ref):
    @pl.when(pl.program_id(0) == 0)
    def _():
        st_ref[...] = jnp.zeros_like(st_ref)

    xb = x_ref[...].astype(BF16)

    def proj(c0, width):
        return _dot(xb, w_ref[:, c0:c0 + width])

    logits = lbl_ref[...]
    m = jnp.max(logits, axis=0, keepdims=True)
    e = jnp.exp(logits - m)
    sm = e / jnp.sum(e, axis=0, keepdims=True)
    csum = sm[0:1]
    for j in range(1, layer + 1):
        csum = csum + sm[j:j + 1]
    lb = csum - sm[0:1]

    def attention_piece(ref, c0, scale):
        def emit():
            ref[...] = (proj(c0, SB_WIDTH) * scale).astype(BF16)
        return emit

    def gate_piece(ref, c0, j):
        def emit():
            ref[:, j:j + HG_WIDTH] = jax.nn.sigmoid(proj(c0 + j, HG_WIDTH)).astype(BF16)
        return emit

    pieces = [attention_piece(qb_ref, _OFF_QB, SB_DH ** -0.5), attention_piece(kb_ref, _OFF_KB, 1.0),
              attention_piece(vb_ref, _OFF_VB, 1.0)]
    pieces += [gate_piece(sga_ref, _OFF_GATE_A, j) for j in range(0, D_MODEL, HG_WIDTH)]
    pieces += [gate_piece(sgb_ref, _OFF_GATE_B, j) for j in range(0, D_MODEL, HG_WIDTH)]
    per_head = -(-len(pieces) // HG_HEADS)

    def rest_of_projection(h):
        def emit():
            for piece in pieces[h * per_head:(h + 1) * per_head]:
                piece()
        return emit

    z = proj(_OFF_FA, HG_WIDTH)
    f_gate = lb + (1.0 - lb) * jax.nn.sigmoid(z)
    _hgrn_rows(jax.nn.silu(proj(_OFF_QA, HG_WIDTH)),
               jnp.log(jnp.maximum(f_gate, F_FLOOR)),
               (1.0 - lb) * jax.nn.sigmoid(-z),
               proj(_OFF_IA, HG_WIDTH),
               jax.nn.silu(proj(_OFF_GA, HG_WIDTH)),
               gn_ref[...], st_ref, oa_ref, [rest_of_projection(h) for h in range(HG_HEADS)])


def _inproj_hgrn(x, w_in, lb_logits, gn, layer, ts):
    s = x.shape[0]
    depth = lb_logits.shape[0]
    row = lambda w: pl.BlockSpec((ts, w), lambda i: (i, 0))
    bf16o = lambda w: jax.ShapeDtypeStruct((s, w), BF16)
    return pl.pallas_call(
        functools.partial(_inproj_kernel, layer),
        grid=(s // ts,),
        in_specs=[row(D_MODEL), _layer_spec(layer, (D_MODEL, IN_COLS)),
                  pl.BlockSpec((depth, HG_WIDTH), lambda i: (0, 0)),
                  _layer_spec(layer, (1, HG_WIDTH))],
        out_specs=[row(HG_WIDTH)] + [row(SB_WIDTH)] * 3 + [row(D_MODEL)] * 2,
        out_shape=[bf16o(HG_WIDTH)] + [bf16o(SB_WIDTH)] * 3 + [bf16o(D_MODEL)] * 2,
        scratch_shapes=[pltpu.VMEM((HG_HEADS, HG_DV, HG_DK), F32)],
        compiler_params=_params("arbitrary"),
        name="inproj_hgrn2",
    )(x, w_in, lb_logits, gn)


SB_QROWS = 64
SB_STEP_ROWS = 1024
SB_FIRST_TILES = 2
SB_LOOP_TILES = 2
SB_GROUP = 16


def _sb_round(qms, lane_masks, tri2, tiles, carries):
    tq = qms[0].shape[0]
    log_fail, log_beta = [], []
    for qm, chain_tiles in zip(qms, tiles):
        for keys, _, mask, valid in chain_tiles:
            z = _dot_nt(qm, keys)
            lg = jnp.log(1.0 + jnp.exp(-jnp.abs(z)))
            lf = jnp.minimum(-z, 0.0) - lg
            log_beta.append(lf + z)
            if mask is not None:
                lf = jnp.where(mask, lf, 0.0)
            if valid is not None:
                lf = lf * valid
            log_fail.append(lf)
    parts = [jnp.concatenate(_split2(lf), axis=1) for lf in log_fail]
    suffix = _dot(jnp.concatenate(parts, axis=0), tri2)

    out = []
    idx = 0
    for n, chain_tiles in enumerate(tiles):
        carry = carries[n]
        ws, vs = [], []
        for _, v, mask, valid in chain_tiles:
            w = jnp.exp(log_beta[idx] + suffix[idx * tq:(idx + 1) * tq, :] + carry)
            if mask is not None:
                w = jnp.where(mask, w, 0.0)
            if valid is not None:
                w = w * valid
            ws.append(w.astype(BF16))
            vs.append(jnp.where(lane_masks[n], v, jnp.zeros_like(v)))
            carry = carry + jnp.sum(log_fail[idx], axis=-1, keepdims=True)
            idx += 1
        out.append((_dot(jnp.concatenate(ws, axis=1), jnp.concatenate(vs, axis=0)), carry))
    return out


def _sb_kernel(q_ref, k_ref, v_ref, o_ref):
    tq, tk = SB_QROWS, SB_BLOCK
    nheads = LANES // SB_DH
    nq = q_ref.shape[0] // tq
    first_row = pl.program_id(1) * q_ref.shape[0]
    lane = lax.broadcasted_iota(jnp.int32, (1, LANES), 1)
    col_minus_row = (lax.broadcasted_iota(jnp.int32, (tq, tk), 1)
                     - lax.broadcasted_iota(jnp.int32, (tq, tk), 0))
    col = lax.broadcasted_iota(jnp.int32, (1, tk), 1)
    r2 = lax.broadcasted_iota(jnp.int32, (2 * tk, tk), 0) & (tk - 1)
    c2 = lax.broadcasted_iota(jnp.int32, (2 * tk, tk), 1)
    tri2 = jnp.where(r2 > c2, 1.0, 0.0).astype(BF16)
    lane_masks = [(lane >> _DH_SHIFT) == h for h in range(nheads)]
    chains = [(a, h) for a in range(nq) for h in range(nheads)]
    qms = []
    for a, h in chains:
        q = q_ref[a * tq:(a + 1) * tq, :]
        qms.append(jnp.where(lane_masks[h], q, jnp.zeros_like(q)))

    def tiles_from(a, offset, count, diagonal_first):
        q0 = first_row + a * tq
        tiles = []
        for t in range(count):
            start = q0 - (tk - tq) - (offset + t) * tk
            lo = jnp.maximum(start, 0)
            rows = pl.ds(pl.multiple_of(lo, tq), tk)
            if diagonal_first and t == 0:
                mask, factor = col_minus_row < q0 - lo, None
            else:
                mask, factor = None, (col < start + tk - lo).astype(F32)
            tiles.append((k_ref[rows, :], v_ref[rows, :], mask, factor))
        return tiles

    def sweep(offset, count, diagonal_first, carries):
        res = []
        for g in range(0, len(chains), SB_GROUP):
            grp = range(g, min(g + SB_GROUP, len(chains)))
            res += _sb_round([qms[n] for n in grp], [lane_masks[chains[n][1]] for n in grp], tri2,
                             [tiles_from(chains[n][0], offset, count, diagonal_first) for n in grp],
                             [carries[n] for n in grp])
        return res

    def cmax_of(carries):
        return jnp.max(functools.reduce(jnp.maximum, carries))

    res = sweep(0, SB_FIRST_TILES, True, [jnp.zeros((tq, 1), F32)] * len(chains))
    accs = tuple(pv for pv, _ in res)
    carries = tuple(c for _, c in res)

    def cond(state):
        offset, _, _, cmax = state
        last_end = first_row + (nq - 1) * tq + tq - offset * tk
        return (last_end > 0) & (cmax > SB_LOG_ZERO)

    def body(state):
        offset, accs, carries, _ = state
        res = sweep(offset, SB_LOOP_TILES, False, carries)
        accs = tuple(acc + pv for acc, (pv, _) in zip(accs, res))
        carries = tuple(c for _, c in res)
        return offset + SB_LOOP_TILES, accs, carries, cmax_of(carries)

    _, accs, _, _ = lax.while_loop(
        cond, body, (jnp.int32(SB_FIRST_TILES), accs, carries, cmax_of(carries)))
    for a in range(nq):
        o_ref[a * tq:(a + 1) * tq, :] = functools.reduce(
            jnp.add, [accs[n] for n, (qa, _) in enumerate(chains) if qa == a]).astype(o_ref.dtype)


def _sb_attention(qb, kb, vb):
    s = qb.shape[0]
    rows = next(r for r in (SB_STEP_ROWS, SB_BLOCK) if s % r == 0)
    kv = pl.BlockSpec((s, LANES), lambda p, i: (0, p))
    qo = pl.BlockSpec((rows, LANES), lambda p, i: (i, p))
    return pl.pallas_call(
        _sb_kernel,
        grid=(SB_WIDTH // LANES, s // rows),
        in_specs=[qo, kv, kv],
        out_specs=qo,
        out_shape=jax.ShapeDtypeStruct((s, SB_WIDTH), BF16),
        compiler_params=_params("arbitrary", "arbitrary"),
        name="stickbreaking",
    )(qb, kb, vb)


SB_WINDOW_PREV = 256
SB_NOTHING_LEFT = -1e30


def _sb_window(q_ref, kprev_ref, kcur_ref, vprev_ref, vcur_ref, tile_row0):
    tq, tk = SB_QROWS, SB_BLOCK
    nheads = LANES // SB_DH
    nq = q_ref.shape[0] // tq
    lane = lax.broadcasted_iota(jnp.int32, (1, LANES), 1)
    col_minus_row = (lax.broadcasted_iota(jnp.int32, (tq, tk), 1)
                     - lax.broadcasted_iota(jnp.int32, (tq, tk), 0))
    col = lax.broadcasted_iota(jnp.int32, (1, tk), 1)
    r2 = lax.broadcasted_iota(jnp.int32, (2 * tk, tk), 0) & (tk - 1)
    c2 = lax.broadcasted_iota(jnp.int32, (2 * tk, tk), 1)
    tri2 = jnp.where(r2 > c2, 1.0, 0.0).astype(BF16)
    lane_masks = [(lane >> _DH_SHIFT) == h for h in range(nheads)]
    chains = [(a, h) for a in range(nq) for h in range(nheads)]
    assert SB_WINDOW_PREV >= (tk - tq) + (SB_FIRST_TILES - 1) * tk

    outs, cmax = [], None
    for p in range(SB_WIDTH // LANES):
        cols = slice(p * LANES, (p + 1) * LANES)
        kwin = jnp.concatenate([kprev_ref[:, cols], kcur_ref[:, cols]], axis=0)
        vwin = jnp.concatenate([vprev_ref[:, cols], vcur_ref[:, cols]], axis=0)
        qms, tiles = [], []
        for a, h in chains:
            q = q_ref[a * tq:(a + 1) * tq, cols]
            qms.append(jnp.where(lane_masks[h], q, jnp.zeros_like(q)))
            chain_tiles = []
            for t in range(SB_FIRST_TILES):
                w = SB_WINDOW_PREV + a * tq - (tk - tq) - t * tk
                live = col >= SB_WINDOW_PREV - w - tile_row0
                if t == 0:
                    mask, factor = (col_minus_row < tk - tq) & live, None
                else:
                    mask, factor = None, live.astype(F32)
                chain_tiles.append((kwin[w:w + tk, :], vwin[w:w + tk, :], mask, factor))
            tiles.append(chain_tiles)
        res = _sb_round(qms, [lane_masks[h] for _, h in chains], tri2, tiles,
                        [jnp.zeros((tq, 1), F32)] * len(chains))
        outs.append(jnp.concatenate(
            [functools.reduce(jnp.add, [res[n][0] for n, (qa, _) in enumerate(chains) if qa == a])
             for a in range(nq)], axis=0).astype(BF16))
        left = []
        for n, (a, _) in enumerate(chains):
            first_key = tile_row0 + a * tq - (tk - tq) - (SB_FIRST_TILES - 1) * tk
            left.append(res[n][1] + jnp.where(first_key > 0, 0.0, SB_NOTHING_LEFT))
        pair_max = jnp.max(functools.reduce(jnp.maximum, left))
        cmax = pair_max if cmax is None else jnp.maximum(cmax, pair_max)
    return jnp.concatenate(outs, axis=1), cmax


def _sb_fast_kernel(q_ref, kprev_ref, kcur_ref, vprev_ref, vcur_ref, o_ref, left_ref):
    ob, cmax = _sb_window(q_ref, kprev_ref, kcur_ref, vprev_ref, vcur_ref,
                          pl.program_id(0) * q_ref.shape[0])
    o_ref[...] = ob
    left_ref[...] = jnp.full(left_ref.shape, cmax, F32)


def _sb_attention_fast(qb, kb, vb, rows):
    s = qb.shape[0]
    tile = pl.BlockSpec((rows, SB_WIDTH), lambda i: (i, 0))
    per_tile = rows // SB_WINDOW_PREV
    prev = pl.BlockSpec((SB_WINDOW_PREV, SB_WIDTH), lambda i: (jnp.maximum(i * per_tile - 1, 0), 0))
    return pl.pallas_call(
        _sb_fast_kernel,
        grid=(s // rows,),
        in_specs=[tile, prev, tile, prev, tile],
        out_specs=[tile, pl.BlockSpec((None, 8, LANES), lambda i: (i, 0, 0))],
        out_shape=[jax.ShapeDtypeStruct((s, SB_WIDTH), BF16),
                   jax.ShapeDtypeStruct((s // rows, 8, LANES), F32)],
        compiler_params=_params("arbitrary"),
        name="stickbreaking_window",
    )(qb, kb, kb, vb, vb)


FF_CHUNK = 256
CARRY_ROWS = 8


def _merge_ffn_kernel(alpha, x_ref, oa_ref, ob_ref, sga_ref, sgb_ref, p_ref,
                      wa_ref, wb_ref, wo_ref, g1_ref, b1_ref,
                      wup_ref, cw_ref, cb_ref, wdn_ref, wpe_ref, wpg_ref, g2_ref, b2_ref,
                      o_ref, carry_ref):
    ts = x_ref.shape[0]

    @pl.when(pl.program_id(0) == 0)
    def _():
        carry_ref[...] = jnp.zeros_like(carry_ref)

    pe = _dot(p_ref[...].astype(BF16), wpe_ref[...])

    ya = _dot(oa_ref[...], wa_ref[...])
    yb = _dot(ob_ref[...], wb_ref[...])
    merged = sga_ref[...].astype(F32) * ya + sgb_ref[...].astype(F32) * yb
    x1 = _layer_norm(alpha * x_ref[...] + _dot(merged.astype(BF16), wo_ref[...]),
                     g1_ref[...], b1_ref[...])
    xb = x1.astype(BF16)
    row = lax.broadcasted_iota(jnp.int32, (ts, FF_CHUNK), 0)

    def up(c0):
        return _dot(xb, wup_ref[:, c0:c0 + FF_CHUNK])

    def conv(u, c0):
        cols = slice(c0, c0 + FF_CHUNK)
        prev = carry_ref[:, cols]
        p1 = prev[CARRY_ROWS - 1:CARRY_ROWS, :]
        p2 = prev[CARRY_ROWS - 2:CARRY_ROWS - 1, :]
        u1 = jnp.where(row == 0, p1, pltpu.roll(u, 1, axis=0))
        u2 = jnp.where(row == 0, p2, jnp.where(row == 1, p1, pltpu.roll(u, 2, axis=0)))
        carry_ref[:, cols] = u[ts - CARRY_ROWS:, :]
        cw = cw_ref[:, cols]
        return cb_ref[:, cols] + cw[0:1, :] * u2 + cw[1:2, :] * u1 + cw[2:3, :] * u

    nchunks = D_FF // FF_CHUNK
    hs = []
    for c in range(nchunks):
        c_val = conv(up(c * FF_CHUNK), c * FF_CHUNK)
        c_gate = conv(up(D_FF + c * FF_CHUNK), D_FF + c * FF_CHUNK)
        hs.append((jax.nn.gelu(c_gate) * c_val).astype(BF16))
    ffn = _dot(jnp.concatenate(hs, axis=1), wdn_ref[...])

    ple = pe * jax.nn.sigmoid(_dot(xb, wpg_ref[...]))
    o_ref[...] = _layer_norm(alpha * x1 + ffn + ple, g2_ref[...], b2_ref[...])


def _merge_ffn(x, oa, ob, sga, sgb, p, bi, w_a, w_b, w_out, ln1_g, ln1_b,
               w_up, conv_w, conv_b, w_down, w_pe, w_pg, ln2_g, ln2_b, alpha, layer, ts):
    s = x.shape[0]
    row = lambda w: pl.BlockSpec((ts, w), lambda i: (i, 0))
    return pl.pallas_call(
        functools.partial(_merge_ffn_kernel, alpha),
        grid=(s // ts,),
        in_specs=[row(D_MODEL), row(HG_WIDTH), row(SB_WIDTH), row(D_MODEL), row(D_MODEL),
                  pl.BlockSpec((None, None, ts, PLE_DIM), lambda i: (layer, bi, i, 0)),
                  _layer_spec(layer, (HG_WIDTH, D_MODEL)), _layer_spec(layer, (SB_WIDTH, D_MODEL)),
                  _layer_spec(layer, (D_MODEL, D_MODEL)), _layer_spec(layer, (1, D_MODEL)),
                  _layer_spec(layer, (1, D_MODEL)),
                  _layer_spec(layer, (D_MODEL, 2 * D_FF)), _layer_spec(layer, (CONV_W, 2 * D_FF)),
                  _layer_spec(layer, (1, 2 * D_FF)), _layer_spec(layer, (D_FF, D_MODEL)),
                  _layer_spec(layer, (PLE_DIM, D_MODEL)), _layer_spec(layer, (D_MODEL, D_MODEL)),
                  _layer_spec(layer, (1, D_MODEL)), _layer_spec(layer, (1, D_MODEL))],
        out_specs=row(D_MODEL),
        out_shape=jax.ShapeDtypeStruct((s, D_MODEL), F32),
        scratch_shapes=[pltpu.VMEM((CARRY_ROWS, 2 * D_FF), F32)],
        compiler_params=_params("arbitrary"),
        name="merge_convffn",
    )(x, oa, ob, sga, sgb, p, w_a, w_b, w_out, ln1_g, ln1_b,
      w_up, conv_w, conv_b, w_down, w_pe, w_pg, ln2_g, ln2_b)


def _tiles(s):
    return next(t for t in (512, 256, SB_BLOCK) if s % t == 0)


def kernel(x, p, lb_logits, w_in, hg_norm_g, w_a, w_b, w_out, ln1_g, ln1_b,
           w_up, conv_w, conv_b, w_down, w_pe, w_pg, ln2_g, ln2_b):
    batch, s, _ = x.shape
    depth = w_in.shape[0]
    assert s % SB_BLOCK == 0 and s % HG_CHUNK == 0
    alpha = float((2 * depth) ** 0.25)
    ts = _tiles(s)
    lbl = lb_logits.astype(F32)
    vec = lambda a: a.reshape(depth, 1, a.shape[-1])
    w_in, w_a, w_b, w_out, w_up, w_down, w_pe, w_pg = (
        w.astype(BF16) for w in (w_in, w_a, w_b, w_out, w_up, w_down, w_pe, w_pg))
    outs = []
    for bi in range(batch):
        h = x[bi]
        for i in range(depth):
            oa, qb, kb, vb, sga, sgb = _inproj_hgrn(h, w_in, lbl, vec(hg_norm_g), i, ts)
            if ts % SB_WINDOW_PREV == 0:
                ob, left = _sb_attention_fast(qb, kb, vb, ts)
                ob = lax.cond(jnp.max(left) > SB_LOG_ZERO,
                              lambda qb=qb, kb=kb, vb=vb: _sb_attention(qb, kb, vb), lambda ob=ob: ob)
            else:
                ob = _sb_attention(qb, kb, vb)
            h = _merge_ffn(h, oa, ob, sga, sgb, p, bi, w_a, w_b, w_out, vec(ln1_g), vec(ln1_b),
                           w_up, conv_w, vec(conv_b), w_down, w_pe, w_pg, vec(ln2_g), vec(ln2_b),
                           alpha, i, ts)
        outs.append(h)
    return jnp.stack(outs, axis=0)
```

```python
import functools

import jax
import jax.numpy as jnp
from jax import lax
from jax.experimental import pallas as pl
from jax.experimental.pallas import tpu as pltpu

F32 = jnp.float32
BF16 = jnp.bfloat16

D_MODEL = 1024
PLE_DIM = 256
HG_HEADS = 4
HG_DK = 128
HG_DV = 128
HG_WIDTH = HG_HEADS * HG_DV
SB_HEADS = 8
SB_DH = 64
SB_WIDTH = SB_HEADS * SB_DH
D_FF = 2816
CONV_W = 3
LN_EPS = 1e-5
RMS_EPS = 1e-6
F_FLOOR = 1e-30

_OFF_QA = 0
_OFF_FA = _OFF_QA + HG_HEADS * HG_DK
_OFF_IA = _OFF_FA + HG_HEADS * HG_DK
_OFF_GA = _OFF_IA + HG_WIDTH
_OFF_QB = _OFF_GA + HG_WIDTH
_OFF_KB = _OFF_QB + SB_WIDTH
_OFF_VB = _OFF_KB + SB_WIDTH
_OFF_GATE_A = _OFF_VB + SB_WIDTH
_OFF_GATE_B = _OFF_GATE_A + D_MODEL
IN_COLS = _OFF_GATE_B + D_MODEL

HG_CHUNK = 64
HG_SUB = 8
SB_BLOCK = 128
LANES = 128
SB_LOG_ZERO = -110.0
_SUB_SHIFT = HG_SUB.bit_length() - 1
_DH_SHIFT = SB_DH.bit_length() - 1

VMEM_LIMIT = 56 * 1024 * 1024


def _params(*sem):
    return pltpu.CompilerParams(dimension_semantics=sem, vmem_limit_bytes=VMEM_LIMIT)


def _layer_spec(layer, shape):
    zeros = (0,) * len(shape)
    return pl.BlockSpec((None,) + tuple(shape), lambda *_: (layer,) + zeros,
                        pipeline_mode=pl.Buffered(1))


def _split2(a):
    hi = a.astype(BF16)
    lo = (a - hi.astype(F32)).astype(BF16)
    return hi, lo


def _dot(a, b):
    return jnp.dot(a, b, preferred_element_type=F32)


def _dot_nt(a, b):
    return lax.dot_general(a, b, (((1,), (1,)), ((), ())), preferred_element_type=F32)


def _dot_tn(a, b):
    return lax.dot_general(a, b, (((0,), (0,)), ((), ())), preferred_element_type=F32)


def _layer_norm(y, g, b):
    mu = jnp.mean(y, axis=-1, keepdims=True)
    yc = y - mu
    var = jnp.mean(yc * yc, axis=-1, keepdims=True)
    return yc * lax.rsqrt(var + LN_EPS) * g + b


def _hgrn_local(q, k, v, b):
    c = HG_CHUNK
    nsub = c // HG_SUB
    b_last = b[c - 1:c, :]

    zeros = lambda n: [jnp.zeros((n, HG_DK), F32)] if n else []
    g3 = lambda a: a.reshape(nsub, HG_SUB, a.shape[-1])
    b3 = g3(b)
    e_own = jnp.broadcast_to(b3[:, HG_SUB - 1:HG_SUB, :], b3.shape).reshape(c, HG_DK)
    khat = k * jnp.exp(e_own - b)
    qparts, kparts = [], []
    for j in range(nsub - 1):
        lo = (j + 1) * HG_SUB
        e_j = b[lo - 1:lo, :]
        qparts.append(jnp.concatenate(
            zeros(lo) + [q[lo:, :] * jnp.exp(b[lo:, :] - e_j)], axis=0))
        kparts.append(jnp.concatenate(
            zeros(lo - HG_SUB) + [khat[lo - HG_SUB:lo, :]] + zeros(c - lo), axis=0))
    a_off = _dot_nt(jnp.concatenate(qparts, axis=1).astype(BF16),
                    jnp.concatenate(kparts, axis=1).astype(BF16))

    q3, k3 = g3(q), g3(k)
    delta = (lax.broadcasted_iota(jnp.int32, (c, c), 0) - lax.broadcasted_iota(jnp.int32, (c, c), 1))
    a_diag = jnp.zeros((c, c), F32)
    for d in range(HG_SUB):
        if d == 0:
            term = q3 * k3
        else:
            bs = pltpu.roll(b3, d, axis=1)
            ks = pltpu.roll(k3, d, axis=1)
            term = q3 * ks * jnp.exp(jnp.minimum(b3 - bs, 0.0))
        a_d = jnp.sum(term.reshape(c, HG_DK), axis=-1, keepdims=True)
        a_diag = jnp.where(delta == d, a_d, a_diag)
    row_sub = lax.broadcasted_iota(jnp.int32, (c, c), 0) >> _SUB_SHIFT
    col_sub = lax.broadcasted_iota(jnp.int32, (c, c), 1) >> _SUB_SHIFT
    a = a_off + jnp.where(row_sub == col_sub, a_diag, 0.0)

    vb = v.astype(BF16)
    o_intra = _dot(a.astype(BF16), vb)
    q_dec = (q * jnp.exp(b)).astype(BF16)
    k_dec = (k * jnp.exp(b_last - b)).astype(BF16)
    return o_intra, q_dec, _dot_tn(vb, k_dec), jnp.exp(b_last)


def _hgrn_rows(q, g, k, v, ga, gn, st_ref, o_ref, before_head):
    c = HG_CHUNK
    nchunks = q.shape[0] // c
    rows = [slice(ci * c, (ci + 1) * c) for ci in range(nchunks)]
    row = lax.broadcasted_iota(jnp.int32, (c, c), 0)
    col = lax.broadcasted_iota(jnp.int32, (c, c), 1)
    tri = jnp.where(col <= row, 1.0, 0.0).astype(BF16)

    bs = []
    for r in rows:
        gh, gl = _split2(g[r, :])
        bs.append(_dot(tri, gh) + _dot(tri, gl))

    for h in range(HG_HEADS):
        cols = slice(h * HG_DK, (h + 1) * HG_DK)
        anchor = before_head[h]()
        local = [_hgrn_local(q[r, cols], k[r, cols], v[r, cols], b[:, cols]) for r, b in zip(rows, bs)]
        st = st_ref[h]
        for r, (o_intra, q_dec, st_inc, st_decay) in zip(rows, local):
            o = o_intra + _dot_nt(q_dec, st.astype(BF16))
            st = st * st_decay + st_inc
            ms = jnp.mean(o * o, axis=-1, keepdims=True)
            out = o * lax.rsqrt(ms + RMS_EPS) * gn[:, cols] * ga[r, cols]
            if r is rows[-1]:
                out = out + anchor
            o_ref[r, cols] = out.astype(o_ref.dtype)
        st_ref[h] = st


def _inproj_kernel(layer, x_ref, w_ref, lbl_ref, gn_ref,
                   oa_ref, qb_ref, kb_ref, vb_ref, sga_ref, sgb_ref, st_ref):
    @pl.when(pl.program_id(0) == 0)
    def _():
        st_ref[...] = jnp.zeros_like(st_ref)

    xb = x_ref[...].astype(BF16)

    def proj(c0, width):
        return _dot(xb, w_ref[:, c0:c0 + width])

    logits = lbl_ref[...]
    m = jnp.max(logits, axis=0, keepdims=True)
    e = jnp.exp(logits - m)
    sm = e / jnp.sum(e, axis=0, keepdims=True)
    csum = sm[0:1]
    for j in range(1, layer + 1):
        csum = csum + sm[j:j + 1]
    lb = csum - sm[0:1]

    def zero_row_of(val):
        bits = pltpu.bitcast(val[val.shape[0] - 8:, val.shape[1] - LANES:], jnp.uint32)
        half = jnp.uint32(16)
        gone = lax.shift_right_logical(lax.shift_right_logical(bits, half), half)
        return pltpu.bitcast(gone, F32)[0:1, :]

    def attention_piece(ref, c0, scale):
        def emit():
            val = proj(c0, SB_WIDTH)
            ref[...] = (val * scale).astype(BF16)
            return zero_row_of(val)
        return emit

    def gate_piece(ref, c0, j):
        def emit():
            val = proj(c0 + j, HG_WIDTH)
            ref[:, j:j + HG_WIDTH] = jax.nn.sigmoid(val).astype(BF16)
            return zero_row_of(val)
        return emit

    pieces = [attention_piece(qb_ref, _OFF_QB, SB_DH ** -0.5), attention_piece(kb_ref, _OFF_KB, 1.0),
              attention_piece(vb_ref, _OFF_VB, 1.0)]
    pieces += [gate_piece(sga_ref, _OFF_GATE_A, j) for j in range(0, D_MODEL, HG_WIDTH)]
    pieces += [gate_piece(sgb_ref, _OFF_GATE_B, j) for j in range(0, D_MODEL, HG_WIDTH)]
    per_head = -(-len(pieces) // HG_HEADS)

    def rest_of_projection(h):
        def emit():
            return functools.reduce(jnp.add, [piece() for piece in pieces[h * per_head:(h + 1) * per_head]])
        return emit

    z = proj(_OFF_FA, HG_WIDTH)
    f_gate = lb + (1.0 - lb) * jax.nn.sigmoid(z)
    _hgrn_rows(jax.nn.silu(proj(_OFF_QA, HG_WIDTH)),
               jnp.log(jnp.maximum(f_gate, F_FLOOR)),
               (1.0 - lb) * jax.nn.sigmoid(-z),
               proj(_OFF_IA, HG_WIDTH),
               jax.nn.silu(proj(_OFF_GA, HG_WIDTH)),
               gn_ref[...], st_ref, oa_ref, [rest_of_projection(h) for h in range(HG_HEADS)])


def _inproj_hgrn(x, w_in, lb_logits, gn, layer, ts):
    s = x.shape[0]
    depth = lb_logits.shape[0]
    row = lambda w: pl.BlockSpec((ts, w), lambda i: (i, 0))
    bf16o = lambda w: jax.ShapeDtypeStruct((s, w), BF16)
    return pl.pallas_call(
        functools.partial(_inproj_kernel, layer),
        grid=(s // ts,),
        in_specs=[row(D_MODEL), _layer_spec(layer, (D_MODEL, IN_COLS)),
                  pl.BlockSpec((depth, HG_WIDTH), lambda i: (0, 0)),
                  _layer_spec(layer, (1, HG_WIDTH))],
        out_specs=[row(HG_WIDTH)] + [row(SB_WIDTH)] * 3 + [row(D_MODEL)] * 2,
        out_shape=[bf16o(HG_WIDTH)] + [bf16o(SB_WIDTH)] * 3 + [bf16o(D_MODEL)] * 2,
        scratch_shapes=[pltpu.VMEM((HG_HEADS, HG_DV, HG_DK), F32)],
        compiler_params=_params("arbitrary"),
        name="inproj_hgrn2",
    )(x, w_in, lb_logits, gn)


SB_QROWS = 64
SB_STEP_ROWS = 1024
SB_FIRST_TILES = 2
SB_LOOP_TILES = 2
SB_GROUP = 16


def _sb_round(qms, lane_masks, tri2, tiles, carries):
    tq = qms[0].shape[0]
    log_fail, log_beta = [], []
    for qm, chain_tiles in zip(qms, tiles):
        for keys, _, mask, valid in chain_tiles:
            z = _dot_nt(qm, keys)
            lg = jnp.log(1.0 + jnp.exp(-jnp.abs(z)))
            lf = jnp.minimum(-z, 0.0) - lg
            log_beta.append(lf + z)
            if mask is not None:
                lf = jnp.where(mask, lf, 0.0)
            if valid is not None:
                lf = lf * valid
            log_fail.append(lf)
    parts = [jnp.concatenate(_split2(lf), axis=1) for lf in log_fail]
    suffix = _dot(jnp.concatenate(parts, axis=0), tri2)

    out = []
    idx = 0
    for n, chain_tiles in enumerate(tiles):
        carry = carries[n]
        ws, vs = [], []
        for _, v, mask, valid in chain_tiles:
            w = jnp.exp(log_beta[idx] + suffix[idx * tq:(idx + 1) * tq, :] + carry)
            if mask is not None:
                w = jnp.where(mask, w, 0.0)
            if valid is not None:
                w = w * valid
            ws.append(w.astype(BF16))
            vs.append(jnp.where(lane_masks[n], v, jnp.zeros_like(v)))
            carry = carry + jnp.sum(log_fail[idx], axis=-1, keepdims=True)
            idx += 1
        out.append((_dot(jnp.concatenate(ws, axis=1), jnp.concatenate(vs, axis=0)), carry))
    return out


def _sb_kernel(q_ref, k_ref, v_ref, o_ref):
    tq, tk = SB_QROWS, SB_BLOCK
    nheads = LANES // SB_DH
    nq = q_ref.shape[0] // tq
    first_row = pl.program_id(1) * q_ref.shape[0]
    lane = lax.broadcasted_iota(jnp.int32, (1, LANES), 1)
    col_minus_row = (lax.broadcasted_iota(jnp.int32, (tq, tk), 1)
                     - lax.broadcasted_iota(jnp.int32, (tq, tk), 0))
    col = lax.broadcasted_iota(jnp.int32, (1, tk), 1)
    r2 = lax.broadcasted_iota(jnp.int32, (2 * tk, tk), 0) & (tk - 1)
    c2 = lax.broadcasted_iota(jnp.int32, (2 * tk, tk), 1)
    tri2 = jnp.where(r2 > c2, 1.0, 0.0).astype(BF16)
    lane_masks = [(lane >> _DH_SHIFT) == h for h in range(nheads)]
    chains = [(a, h) for a in range(nq) for h in range(nheads)]
    qms = []
    for a, h in chains:
        q = q_ref[a * tq:(a + 1) * tq, :]
        qms.append(jnp.where(lane_masks[h], q, jnp.zeros_like(q)))

    def tiles_from(a, offset, count, diagonal_first):
        q0 = first_row + a * tq
        tiles = []
        for t in range(count):
            start = q0 - (tk - tq) - (offset + t) * tk
            lo = jnp.maximum(start, 0)
            rows = pl.ds(pl.multiple_of(lo, tq), tk)
            if diagonal_first and t == 0:
                mask, factor = col_minus_row < q0 - lo, None
            else:
                mask, factor = None, (col < start + tk - lo).astype(F32)
            tiles.append((k_ref[rows, :], v_ref[rows, :], mask, factor))
        return tiles

    def sweep(offset, count, diagonal_first, carries):
        res = []
        for g in range(0, len(chains), SB_GROUP):
            grp = range(g, min(g + SB_GROUP, len(chains)))
            res += _sb_round([qms[n] for n in grp], [lane_masks[chains[n][1]] for n in grp], tri2,
                             [tiles_from(chains[n][0], offset, count, diagonal_first) for n in grp],
                             [carries[n] for n in grp])
        return res

    def cmax_of(carries):
        return jnp.max(functools.reduce(jnp.maximum, carries))

    res = sweep(0, SB_FIRST_TILES, True, [jnp.zeros((tq, 1), F32)] * len(chains))
    accs = tuple(pv for pv, _ in res)
    carries = tuple(c for _, c in res)

    def cond(state):
        offset, _, _, cmax = state
        last_end = first_row + (nq - 1) * tq + tq - offset * tk
        return (last_end > 0) & (cmax > SB_LOG_ZERO)

    def body(state):
        offset, accs, carries, _ = state
        res = sweep(offset, SB_LOOP_TILES, False, carries)
        accs = tuple(acc + pv for acc, (pv, _) in zip(accs, res))
        carries = tuple(c for _, c in res)
        return offset + SB_LOOP_TILES, accs, carries, cmax_of(carries)

    _, accs, _, _ = lax.while_loop(
        cond, body, (jnp.int32(SB_FIRST_TILES), accs, carries, cmax_of(carries)))
    for a in range(nq):
        o_ref[a * tq:(a + 1) * tq, :] = functools.reduce(
            jnp.add, [accs[n] for n, (qa, _) in enumerate(chains) if qa == a]).astype(o_ref.dtype)


def _sb_attention(qb, kb, vb):
    s = qb.shape[0]
    rows = next(r for r in (SB_STEP_ROWS, SB_BLOCK) if s % r == 0)
    kv = pl.BlockSpec((s, LANES), lambda p, i: (0, p))
    qo = pl.BlockSpec((rows, LANES), lambda p, i: (i, p))
    return pl.pallas_call(
        _sb_kernel,
        grid=(SB_WIDTH // LANES, s // rows),
        in_specs=[qo, kv, kv],
        out_specs=qo,
        out_shape=jax.ShapeDtypeStruct((s, SB_WIDTH), BF16),
        compiler_params=_params("arbitrary", "arbitrary"),
        name="stickbreaking",
    )(qb, kb, vb)


SB_WINDOW_PREV = 256
SB_NOTHING_LEFT = -1e30


def _sb_window(q_ref, kprev_ref, kcur_ref, vprev_ref, vcur_ref, tile_row0):
    tq, tk = SB_QROWS, SB_BLOCK
    nheads = LANES // SB_DH
    nq = q_ref.shape[0] // tq
    lane = lax.broadcasted_iota(jnp.int32, (1, LANES), 1)
    col_minus_row = (lax.broadcasted_iota(jnp.int32, (tq, tk), 1)
                     - lax.broadcasted_iota(jnp.int32, (tq, tk), 0))
    col = lax.broadcasted_iota(jnp.int32, (1, tk), 1)
    r2 = lax.broadcasted_iota(jnp.int32, (2 * tk, tk), 0) & (tk - 1)
    c2 = lax.broadcasted_iota(jnp.int32, (2 * tk, tk), 1)
    tri2 = jnp.where(r2 > c2, 1.0, 0.0).astype(BF16)
    lane_masks = [(lane >> _DH_SHIFT) == h for h in range(nheads)]
    chains = [(a, h) for a in range(nq) for h in range(nheads)]
    assert SB_WINDOW_PREV >= (tk - tq) + (SB_FIRST_TILES - 1) * tk

    outs, cmax = [], None
    for p in range(SB_WIDTH // LANES):
        cols = slice(p * LANES, (p + 1) * LANES)
        kwin = jnp.concatenate([kprev_ref[:, cols], kcur_ref[:, cols]], axis=0)
        vwin = jnp.concatenate([vprev_ref[:, cols], vcur_ref[:, cols]], axis=0)
        qms, tiles = [], []
        for a, h in chains:
            q = q_ref[a * tq:(a + 1) * tq, cols]
            qms.append(jnp.where(lane_masks[h], q, jnp.zeros_like(q)))
            chain_tiles = []
            for t in range(SB_FIRST_TILES):
                w = SB_WINDOW_PREV + a * tq - (tk - tq) - t * tk
                live = col >= SB_WINDOW_PREV - w - tile_row0
                if t == 0:
                    mask, factor = (col_minus_row < tk - tq) & live, None
                else:
                    mask, factor = None, live.astype(F32)
                chain_tiles.append((kwin[w:w + tk, :], vwin[w:w + tk, :], mask, factor))
            tiles.append(chain_tiles)
        res = _sb_round(qms, [lane_masks[h] for _, h in chains], tri2, tiles,
                        [jnp.zeros((tq, 1), F32)] * len(chains))
        outs.append(jnp.concatenate(
            [functools.reduce(jnp.add, [res[n][0] for n, (qa, _) in enumerate(chains) if qa == a])
             for a in range(nq)], axis=0).astype(BF16))
        left = []
        for n, (a, _) in enumerate(chains):
            first_key = tile_row0 + a * tq - (tk - tq) - (SB_FIRST_TILES - 1) * tk
            left.append(res[n][1] + jnp.where(first_key > 0, 0.0, SB_NOTHING_LEFT))
        pair_max = jnp.max(functools.reduce(jnp.maximum, left))
        cmax = pair_max if cmax is None else jnp.maximum(cmax, pair_max)
    return jnp.concatenate(outs, axis=1), cmax


def _sb_fast_kernel(q_ref, kprev_ref, kcur_ref, vprev_ref, vcur_ref, o_ref, left_ref):
    ob, cmax = _sb_window(q_ref, kprev_ref, kcur_ref, vprev_ref, vcur_ref,
                          pl.program_id(0) * q_ref.shape[0])
    o_ref[...] = ob
    left_ref[...] = jnp.full(left_ref.shape, cmax, F32)


def _sb_attention_fast(qb, kb, vb, rows):
    s = qb.shape[0]
    tile = pl.BlockSpec((rows, SB_WIDTH), lambda i: (i, 0))
    per_tile = rows // SB_WINDOW_PREV
    prev = pl.BlockSpec((SB_WINDOW_PREV, SB_WIDTH), lambda i: (jnp.maximum(i * per_tile - 1, 0), 0))
    return pl.pallas_call(
        _sb_fast_kernel,
        grid=(s // rows,),
        in_specs=[tile, prev, tile, prev, tile],
        out_specs=[tile, pl.BlockSpec((None, 8, LANES), lambda i: (i, 0, 0))],
        out_shape=[jax.ShapeDtypeStruct((s, SB_WIDTH), BF16),
                   jax.ShapeDtypeStruct((s // rows, 8, LANES), F32)],
        compiler_params=_params("arbitrary"),
        name="stickbreaking_window",
    )(qb, kb, kb, vb, vb)


FF_CHUNK = 256
CARRY_ROWS = 8


def _merge_ffn_kernel(alpha, x_ref, oa_ref, ob_ref, sga_ref, sgb_ref, p_ref,
                      wa_ref, wb_ref, wo_ref, g1_ref, b1_ref,
                      wup_ref, cw_ref, cb_ref, wdn_ref, wpe_ref, wpg_ref, g2_ref, b2_ref,
                      o_ref, carry_ref):
    ts = x_ref.shape[0]

    @pl.when(pl.program_id(0) == 0)
    def _():
        carry_ref[...] = jnp.zeros_like(carry_ref)

    pe = _dot(p_ref[...].astype(BF16), wpe_ref[...])

    ya = _dot(oa_ref[...], wa_ref[...])
    yb = _dot(ob_ref[...], wb_ref[...])
    merged = sga_ref[...].astype(F32) * ya + sgb_ref[...].astype(F32) * yb
    x1 = _layer_norm(alpha * x_ref[...] + _dot(merged.astype(BF16), wo_ref[...]),
                     g1_ref[...], b1_ref[...])
    xb = x1.astype(BF16)
    row = lax.broadcasted_iota(jnp.int32, (ts, FF_CHUNK), 0)

    def up(c0):
        return _dot(xb, wup_ref[:, c0:c0 + FF_CHUNK])

    def conv(u, c0):
        cols = slice(c0, c0 + FF_CHUNK)
        prev = carry_ref[:, cols]
        p1 = prev[CARRY_ROWS - 1:CARRY_ROWS, :]
        p2 = prev[CARRY_ROWS - 2:CARRY_ROWS - 1, :]
        u1 = jnp.where(row == 0, p1, pltpu.roll(u, 1, axis=0))
        u2 = jnp.where(row == 0, p2, jnp.where(row == 1, p1, pltpu.roll(u, 2, axis=0)))
        carry_ref[:, cols] = u[ts - CARRY_ROWS:, :]
        cw = cw_ref[:, cols]
        return cb_ref[:, cols] + cw[0:1, :] * u2 + cw[1:2, :] * u1 + cw[2:3, :] * u

    nchunks = D_FF // FF_CHUNK
    hs = []
    for c in range(nchunks):
        c_val = conv(up(c * FF_CHUNK), c * FF_CHUNK)
        c_gate = conv(up(D_FF + c * FF_CHUNK), D_FF + c * FF_CHUNK)
        hs.append((jax.nn.gelu(c_gate) * c_val).astype(BF16))
    ffn = _dot(jnp.concatenate(hs, axis=1), wdn_ref[...])

    ple = pe * jax.nn.sigmoid(_dot(xb, wpg_ref[...]))
    o_ref[...] = _layer_norm(alpha * x1 + ffn + ple, g2_ref[...], b2_ref[...])


def _merge_ffn(x, oa, ob, sga, sgb, p, bi, w_a, w_b, w_out, ln1_g, ln1_b,
               w_up, conv_w, conv_b, w_down, w_pe, w_pg, ln2_g, ln2_b, alpha, layer, ts):
    s = x.shape[0]
    row = lambda w: pl.BlockSpec((ts, w), lambda i: (i, 0))
    return pl.pallas_call(
        functools.partial(_merge_ffn_kernel, alpha),
        grid=(s // ts,),
        in_specs=[row(D_MODEL), row(HG_WIDTH), row(SB_WIDTH), row(D_MODEL), row(D_MODEL),
                  pl.BlockSpec((None, None, ts, PLE_DIM), lambda i: (layer, bi, i, 0)),
                  _layer_spec(layer, (HG_WIDTH, D_MODEL)), _layer_spec(layer, (SB_WIDTH, D_MODEL)),
                  _layer_spec(layer, (D_MODEL, D_MODEL)), _layer_spec(layer, (1, D_MODEL)),
                  _layer_spec(layer, (1, D_MODEL)),
                  _layer_spec(layer, (D_MODEL, 2 * D_FF)), _layer_spec(layer, (CONV_W, 2 * D_FF)),
                  _layer_spec(layer, (1, 2 * D_FF)), _layer_spec(layer, (D_FF, D_MODEL)),
                  _layer_spec(layer, (PLE_DIM, D_MODEL)), _layer_spec(layer, (D_MODEL, D_MODEL)),
                  _layer_spec(layer, (1, D_MODEL)), _layer_spec(layer, (1, D_MODEL))],
        out_specs=row(D_MODEL),
        out_shape=jax.ShapeDtypeStruct((s, D_MODEL), F32),
        scratch_shapes=[pltpu.VMEM((CARRY_ROWS, 2 * D_FF), F32)],
        compiler_params=_params("arbitrary"),
        name="merge_convffn",
    )(x, oa, ob, sga, sgb, p, w_a, w_b, w_out, ln1_g, ln1_b,
      w_up, conv_w, conv_b, w_down, w_pe, w_pg, ln2_g, ln2_b)


def _tiles(s):
    return next(t for t in (512, 256, SB_BLOCK) if s % t == 0)


def kernel(x, p, lb_logits, w_in, hg_norm_g, w_a, w_b, w_out, ln1_g, ln1_b,
           w_up, conv_w, conv_b, w_down, w_pe, w_pg, ln2_g, ln2_b):
    batch, s, _ = x.shape
    depth = w_in.shape[0]
    assert s % SB_BLOCK == 0 and s % HG_CHUNK == 0
    alpha = float((2 * depth) ** 0.25)
    ts = _tiles(s)
    lbl = lb_logits.astype(F32)
    vec = lambda a: a.reshape(depth, 1, a.shape[-1])
    w_in, w_a, w_b, w_out, w_up, w_down, w_pe, w_pg = (
        w.astype(BF16) for w in (w_in, w_a, w_b, w_out, w_up, w_down, w_pe, w_pg))
    outs = []
    for bi in range(batch):
        h = x[bi]
        for i in range(depth):
            oa, qb, kb, vb, sga, sgb = _inproj_hgrn(h, w_in, lbl, vec(hg_norm_g), i, ts)
            if ts % SB_WINDOW_PREV == 0:
                ob, left = _sb_attention_fast(qb, kb, vb, ts)
                ob = lax.cond(jnp.max(left) > SB_LOG_ZERO,
                              lambda qb=qb, kb=kb, vb=vb: _sb_attention(qb, kb, vb), lambda ob=ob: ob)
            else:
                ob = _sb_attention(qb, kb, vb)
            h = _merge_ffn(h, oa, ob, sga, sgb, p, bi, w_a, w_b, w_out, vec(ln1_g), vec(ln1_b),
                           w_up, conv_w, vec(conv_b), w_down, w_pe, w_pg, vec(ln2_g), vec(ln2_b),
                           alpha, i, ts)
        outs.append(h)
    return jnp.stack(outs, axis=0)
```
